```python
import jax, jax.numpy as jnp
from jax import lax
import numpy as np

D_MODEL = 1024
BATCH = 2
SEQ = 8192
DEPTH = 1
DEC_BATCH = 16
DEC_SEQ = 16
PAST_LEN = 1024

CHUNK = 64
N_HEADS = 8
HEAD_DIM = 128
ATTN_WIDTH = N_HEADS * HEAD_DIM
N_IDX_HEADS = 8
IDX_DIM = 64
TOPK_MAX = 256
CONV_WIDTH = 1024
CONV_K = 3
QBLOCK = 128
PROJ_DIM = 4 * ATTN_WIDTH + N_IDX_HEADS * IDX_DIM + IDX_DIM + N_IDX_HEADS + 4 * CONV_WIDTH + 2 * D_MODEL
EPS = 1e-6
NEG_INF = -1e30

kernel_name = "hybrid_dsa_shortconv_streaming_step"


def rms_norm(x, g):
    xf = x.astype(jnp.float32)
    y = xf * lax.rsqrt(jnp.mean(xf * xf, axis=-1, keepdims=True) + EPS)
    return (y * g.astype(jnp.float32)).astype(x.dtype)


def dsa_query_block(q, qi, wi, q_pos, k, v, ki, k_pos, k_top):
    dots = jnp.einsum('bqhd,bld->bqhl', qi, ki).astype(jnp.float32) * (IDX_DIM ** -0.5)
    score = jnp.einsum('bqhl,bqh->bql', jax.nn.relu(dots), wi.astype(jnp.float32)) * (N_IDX_HEADS ** -0.5)
    admissible = (k_pos[None, :] // CHUNK) <= (q_pos[:, None] // CHUNK)
    score = jnp.where(admissible[None], score, NEG_INF)
    _, idx = lax.top_k(score, k_top)
    sel_ok = jnp.take_along_axis(jnp.broadcast_to(admissible[None], score.shape), idx, axis=-1)
    gather = jax.vmap(lambda rows, ids: rows[ids])
    k_sel = gather(k, idx)
    v_sel = gather(v, idx)
    logits = jnp.einsum('bqhd,bqkhd->bqhk', q, k_sel).astype(jnp.float32) * (HEAD_DIM ** -0.5)
    logits = jnp.where(sel_ok[:, :, None, :], logits, NEG_INF)
    p = jax.nn.softmax(logits, axis=-1).astype(v.dtype)
    return jnp.einsum('bqhk,bqkhd->bqhd', p, v_sel)


def dsa_attention(q, qi, wi, k, v, ki, past_len, k_top):
    B, T = q.shape[0], q.shape[1]
    L = k.shape[1]
    qb = min(QBLOCK, T)
    nb = T // qb

    def to_blocks(a):
        return jnp.moveaxis(a.reshape((B, nb, qb) + a.shape[2:]), 1, 0)

    q_pos = (past_len + jnp.arange(T, dtype=jnp.int32)).reshape(nb, qb)
    k_pos = jnp.arange(L, dtype=jnp.int32)

    def one_block(args):
        q_b, qi_b, wi_b, pos_b = args
        return dsa_query_block(q_b, qi_b, wi_b, pos_b, k, v, ki, k_pos, k_top)

    out = lax.map(one_block, (to_blocks(q), to_blocks(qi), to_blocks(wi), q_pos))
    return jnp.moveaxis(out, 0, 1).reshape(q.shape)


def hybrid_layer(x, k_past, v_past, ki_past, conv_past,
                 norm_pre, w_in, conv_w, w_a_out, w_b_out, b_merge, w_out, norm_post):
    B, T, _ = x.shape
    P = k_past.shape[1]
    h = rms_norm(x, norm_pre)
    proj = h @ w_in
    sizes = [ATTN_WIDTH] * 4 + [N_IDX_HEADS * IDX_DIM, IDX_DIM, N_IDX_HEADS] + [CONV_WIDTH] * 4 + [D_MODEL] * 2
    offs = np.cumsum(sizes)[:-1].tolist()
    q, k, v, z_a, qi, ki, wi, u, gate_b, gate_c, z_b, g_a, g_b = jnp.split(proj, offs, axis=-1)
    q = q.reshape(B, T, N_HEADS, HEAD_DIM)
    k = k.reshape(B, T, N_HEADS, HEAD_DIM)
    v = v.reshape(B, T, N_HEADS, HEAD_DIM)
    qi = qi.reshape(B, T, N_IDX_HEADS, IDX_DIM)

    k_all = jnp.concatenate([k_past, k], axis=1)
    v_all = jnp.concatenate([v_past, v], axis=1)
    ki_all = jnp.concatenate([ki_past, ki], axis=1)
    L = P + T
    k_top = min(TOPK_MAX, L // 4)
    attn = dsa_attention(q, qi, wi, k_all, v_all, ki_all, P, k_top)
    y_a = (jax.nn.silu(z_a) * attn.reshape(B, T, ATTN_WIDTH)) @ w_a_out

    cin = gate_c * u
    cpad = jnp.concatenate([conv_past, cin], axis=1)
    conv = sum(conv_w[j] * cpad[:, j:j + T] for j in range(CONV_K))
    y_b = (jax.nn.silu(z_b) * gate_b * conv) @ w_b_out

    merged = jax.nn.sigmoid(g_a + b_merge[0]) * y_a + jax.nn.sigmoid(g_b + b_merge[1]) * y_b
    out = merged @ w_out
    y = x + rms_norm(out, norm_post)
    return y, k, v, ki, cpad[:, T:]


def setup_inputs(seed: int = 0) -> dict:
    key = jax.random.key(seed)
    ks = jax.random.split(key, 16)
    f32 = jnp.float32
    nrm = lambda k, s, sc: jax.random.normal(k, s, f32) * sc
    return {
        "x_prompt": nrm(ks[0], (BATCH, SEQ, D_MODEL), 1.0),
        "x_sample": nrm(ks[1], (DEC_BATCH, DEC_SEQ, D_MODEL), 1.0),
        "cache_k": nrm(ks[2], (DEPTH, DEC_BATCH, PAST_LEN, N_HEADS, HEAD_DIM), 1.0),
        "cache_v": nrm(ks[3], (DEPTH, DEC_BATCH, PAST_LEN, N_HEADS, HEAD_DIM), 1.0),
        "cache_kidx": nrm(ks[4], (DEPTH, DEC_BATCH, PAST_LEN, IDX_DIM), 1.0),
        "state_conv": nrm(ks[5], (DEPTH, DEC_BATCH, CONV_K - 1, CONV_WIDTH), 1.0),
        "norm_pre": 1.0 + nrm(ks[6], (DEPTH, D_MODEL), 0.05),
        "w_in": nrm(ks[7], (DEPTH, D_MODEL, PROJ_DIM), D_MODEL ** -0.5),
        "conv_w": nrm(ks[8], (DEPTH, CONV_K, CONV_WIDTH), CONV_K ** -0.5),
        "w_a_out": nrm(ks[9], (DEPTH, ATTN_WIDTH, D_MODEL), ATTN_WIDTH ** -0.5),
        "w_b_out": nrm(ks[10], (DEPTH, CONV_WIDTH, D_MODEL), CONV_WIDTH ** -0.5),
        "b_merge": nrm(ks[11], (DEPTH, 2, D_MODEL), 0.02),
        "w_out": nrm(ks[12], (DEPTH, D_MODEL, D_MODEL), D_MODEL ** -0.5),
        "norm_post": 1.0 + nrm(ks[13], (DEPTH, D_MODEL), 0.05),
    }


def reference(x_prompt, x_sample, cache_k, cache_v, cache_kidx, state_conv,
              norm_pre, w_in, conv_w, w_a_out, w_b_out, b_merge, w_out, norm_post):
    B = x_prompt.shape[0]
    hp, hs = x_prompt, x_sample
    kp, vp, kip, cp, ksl, vsl, kisl, csl = [], [], [], [], [], [], [], []
    for layer in range(DEPTH):
        w = (norm_pre[layer], w_in[layer], conv_w[layer], w_a_out[layer], w_b_out[layer],
             b_merge[layer], w_out[layer], norm_post[layer])
        empty_kv = jnp.zeros((B, 0, N_HEADS, HEAD_DIM), hp.dtype)
        empty_ki = jnp.zeros((B, 0, IDX_DIM), hp.dtype)
        zero_conv = jnp.zeros((B, CONV_K - 1, CONV_WIDTH), hp.dtype)
        hp, k1, v1, ki1, c1 = hybrid_layer(hp, empty_kv, empty_kv, empty_ki, zero_conv, *w)
        hs, k2, v2, ki2, c2 = hybrid_layer(hs, cache_k[layer], cache_v[layer], cache_kidx[layer],
                                           state_conv[layer], *w)
        kp.append(k1); vp.append(v1); kip.append(ki1); cp.append(c1)
        ksl.append(k2); vsl.append(v2); kisl.append(ki2); csl.append(c2)
    return (hp, hs, jnp.stack(kp), jnp.stack(vp), jnp.stack(kip), jnp.stack(cp),
            jnp.stack(ksl), jnp.stack(vsl), jnp.stack(kisl), jnp.stack(csl))
```

```python
import functools

import numpy as np
import jax
import jax.numpy as jnp
from jax import lax
from jax.experimental import pallas as pl
from jax.experimental.pallas import tpu as pltpu

D_MODEL = 1024
CHUNK = 64
N_HEADS = 8
HEAD_DIM = 128
ATTN_WIDTH = N_HEADS * HEAD_DIM
N_IDX_HEADS = 8
IDX_DIM = 64
TOPK_MAX = 256
CONV_WIDTH = 1024
CONV_K = 3
EPS = 1e-6
NEG_INF = -1e30

LANES = 128
VMEM_LIMIT_BYTES = 60 * 1024 * 1024

INT_MIN = -(2 ** 31)
KEY_NEG_INF = -int(np.float32(NEG_INF).view(np.int32) & 0x7FFFFFFF)
KEY_INADMISSIBLE = KEY_NEG_INF - 1

F32 = jnp.float32
BF16 = jnp.bfloat16
I32 = jnp.int32


def _resident(shape, index_map):
    return pl.BlockSpec(shape, index_map, pipeline_mode=pl.Buffered(1))


def _proj_kernel(x_ref, past_ref, npre_ref, wqkvz_ref, wqi_ref, wki_ref, wwi_ref, wconv_ref,
                 wg_ref, convw_ref, wbout_ref, bmerge_ref,
                 k_ref, v_ref, ki_ref, cstate_ref, qb_ref, kb_ref, vb_ref, qib_ref, kib_ref,
                 wi_ref, sza_ref, ga_ref, gyb_ref, carry_ref, *, groups, rows, n_row_tiles):
    t = pl.program_id(1)
    x = x_ref[0]
    ms = jnp.mean(x * x, axis=-1, keepdims=True)
    h = ((x * lax.rsqrt(ms + EPS)) * npre_ref[...]).astype(BF16)

    def proj(w_ref, lo, hi):
        return jnp.dot(h, w_ref[:, lo:hi], preferred_element_type=F32)

    W = ATTN_WIDTH
    q = proj(wqkvz_ref, 0, W)
    qb_ref[0] = q.astype(BF16)
    k = proj(wqkvz_ref, W, 2 * W)
    k_ref[0] = k
    kb_ref[0] = k.astype(BF16)
    v = proj(wqkvz_ref, 2 * W, 3 * W)
    v_ref[0] = v
    vb_ref[0] = v.astype(BF16)
    z_a = proj(wqkvz_ref, 3 * W, 4 * W)
    sza_ref[0] = z_a * jax.nn.sigmoid(z_a)
    qib_ref[0] = proj(wqi_ref, 0, N_IDX_HEADS * LANES).astype(BF16)
    ki = proj(wki_ref, 0, LANES)
    ki_ref[0] = ki[:, :IDX_DIM]
    kib_ref[0] = ki.astype(BF16)
    wi_ref[0] = proj(wwi_ref, 0, LANES)[:, :N_IDX_HEADS]

    C = CONV_WIDTH
    u = proj(wconv_ref, 0, C)
    gate_c = proj(wconv_ref, 2 * C, 3 * C)
    cin = gate_c * u
    gate_b = proj(wconv_ref, C, 2 * C)
    z_b = proj(wconv_ref, 3 * C, 4 * C)
    pre = (z_b * jax.nn.sigmoid(z_b)) * gate_b

    @pl.when(t == 0)
    def _():
        carry_ref[...] = past_ref[...]

    w0 = convw_ref[0:1, :]
    w1 = convw_ref[1:2, :]
    w2 = convw_ref[2:3, :]
    row = lax.broadcasted_iota(I32, (rows, C), 0)
    convs = []
    for g in range(groups):
        cg = cin[g * rows:(g + 1) * rows]
        past0 = carry_ref[g, 0:1, :]
        past1 = carry_ref[g, 1:2, :]
        r1 = jnp.where(row == 0, past1, pltpu.roll(cg, 1, 0))
        r2 = jnp.where(row == 0, past0, jnp.where(row == 1, past1, pltpu.roll(cg, 2, 0)))
        convs.append(w0 * r2 + w1 * r1 + w2 * cg)
        carry_ref[g] = cg[rows - (CONV_K - 1):rows]
    conv = convs[0] if groups == 1 else jnp.concatenate(convs, axis=0)

    @pl.when(t == n_row_tiles - 1)
    def _():
        cstate_ref[...] = carry_ref[...]

    y_b = jnp.dot((pre * conv).astype(BF16), wbout_ref[...], preferred_element_type=F32)
    g_a = proj(wg_ref, 0, D_MODEL)
    ga_ref[0] = jax.nn.sigmoid(g_a + bmerge_ref[0:1, :])
    g_b = proj(wg_ref, D_MODEL, 2 * D_MODEL)
    gyb_ref[0] = jax.nn.sigmoid(g_b + bmerge_ref[1:2, :]) * y_b


def _proj_call(x, conv_past, w, *, groups, rows, n_row_tiles):
    nb, t_all, d = x.shape
    m = groups * rows
    assert t_all == n_row_tiles * m and (groups == 1 or n_row_tiles == 1)
    row_blk = lambda width: pl.BlockSpec((1, m, width), lambda b, t: (b, t, 0))
    full = lambda a: _resident(a.shape, lambda b, t: (0,) * a.ndim)
    state_blk = pl.BlockSpec((groups, CONV_K - 1, CONV_WIDTH), lambda b, t: (b, 0, 0))
    f32_out = lambda width: jax.ShapeDtypeStruct((nb, t_all, width), F32)
    bf_out = lambda width: jax.ShapeDtypeStruct((nb, t_all, width), BF16)
    weights = (w["norm_pre"], w["w_qkvz"], w["w_qi"], w["w_ki"], w["w_wi"], w["w_conv"], w["w_g"],
               w["conv_w"], w["w_b_out"], w["b_merge"])
    out_shapes = (
        f32_out(ATTN_WIDTH), f32_out(ATTN_WIDTH), f32_out(IDX_DIM),
        jax.ShapeDtypeStruct(conv_past.shape, F32),
        bf_out(ATTN_WIDTH), bf_out(ATTN_WIDTH), bf_out(ATTN_WIDTH),
        bf_out(N_IDX_HEADS * LANES), bf_out(LANES), f32_out(N_IDX_HEADS),
        f32_out(ATTN_WIDTH), f32_out(D_MODEL), f32_out(D_MODEL),
    )
    out_specs = (
        row_blk(ATTN_WIDTH), row_blk(ATTN_WIDTH), row_blk(IDX_DIM), state_blk,
        row_blk(ATTN_WIDTH), row_blk(ATTN_WIDTH), row_blk(ATTN_WIDTH),
        row_blk(N_IDX_HEADS * LANES), row_blk(LANES), row_blk(N_IDX_HEADS),
        row_blk(ATTN_WIDTH), row_blk(D_MODEL), row_blk(D_MODEL),
    )
    return pl.pallas_call(
        functools.partial(_proj_kernel, groups=groups, rows=rows, n_row_tiles=n_row_tiles),
        grid=(nb, n_row_tiles),
        in_specs=[row_blk(d), state_blk] + [full(a) for a in weights],
        out_specs=out_specs,
        out_shape=out_shapes,
        scratch_shapes=[pltpu.VMEM((groups, CONV_K - 1, CONV_WIDTH), F32)],
        compiler_params=pltpu.CompilerParams(
            dimension_semantics=("arbitrary", "arbitrary"), vmem_limit_bytes=VMEM_LIMIT_BYTES),
        name="proj",
    )(x, conv_past, *weights)


def _attn_kernel(qb_ref, qib_ref, wi_ref, k_ref, v_ref, kib_ref, sza_ref, ga_ref, gyb_ref, x_ref,
                 waout_ref, wout_ref, npost_ref, y_ref,
                 key_ref, acc_ref, m_ref, l_ref, *, q_rows, tk, past_len, n_keys, k_top):
    i = pl.program_id(1)
    Q = q_rows
    n_key_pad = key_ref.shape[1]
    qpos = past_len + i * Q + lax.broadcasted_iota(I32, (Q, 1), 0)
    lim = jnp.minimum((qpos // CHUNK + 1) * CHUNK, n_keys)
    last_lim = jnp.minimum(((past_len + (i + 1) * Q - 1) // CHUNK + 1) * CHUNK, n_keys)
    n_tiles = (last_lim + tk - 1) // tk
    col0 = lax.broadcasted_iota(I32, (Q, tk), 1)

    wsc = wi_ref[0] * (IDX_DIM ** -0.5 * N_IDX_HEADS ** -0.5)

    def score_tile(j, carry):
        off = pl.multiple_of(j * tk, tk)
        kt = kib_ref[0, pl.ds(off, tk), :]
        s = jnp.zeros((Q, tk), F32)
        for hh in range(N_IDX_HEADS):
            qh = qib_ref[0, :, hh * LANES:(hh + 1) * LANES]
            d = lax.dot_general(qh, kt, (((1,), (1,)), ((), ())), preferred_element_type=F32)
            s = s + jnp.maximum(d, 0.0) * wsc[:, hh:hh + 1]
        bits = pltpu.bitcast(s, I32)
        key = jnp.where(bits < 0, INT_MIN - bits, bits)
        key = jnp.where(key < KEY_NEG_INF, key - 1, key)
        col = col0 + off
        key = jnp.where(col < lim, key, KEY_INADMISSIBLE)
        if n_key_pad != n_keys:
            key = jnp.where(col < n_keys, key, INT_MIN)
        key_ref[:, pl.ds(off, tk)] = key
        return carry

    lax.fori_loop(0, n_tiles, score_tile, 0)

    def count(pred):
        def body(j, c):
            off = pl.multiple_of(j * tk, tk)
            hit = jnp.where(pred(key_ref[:, pl.ds(off, tk)], col0 + off), 1, 0)
            for g in range(tk // LANES):
                c = c + hit[:, g * LANES:(g + 1) * LANES]
            return c
        c = lax.fori_loop(0, n_tiles, body, jnp.zeros((Q, LANES), I32))
        return jnp.sum(c, axis=1, keepdims=True)

    def count_ge(cand):
        return count(lambda key, col: key >= cand)

    thr = jnp.where(count_ge(jnp.zeros((Q, 1), I32)) >= k_top, 0, INT_MIN).astype(I32)

    def bit_step(b, thr):
        cand = thr | lax.shift_left(jnp.int32(1), 30 - b)
        return jnp.where(count_ge(cand) >= k_top, cand, thr)

    thr = lax.fori_loop(0, 31, bit_step, thr)

    n_ge = count_ge(thr)
    need_fix = jnp.max(jnp.where((n_ge > k_top) | (thr <= KEY_INADMISSIBLE), 1, 0)) > 0

    @pl.when(need_fix)
    def _():
        n_gt = count_ge(thr + 1)
        want = k_top - n_gt

        def idx_step(b, a):
            cand = a | lax.shift_left(jnp.int32(1), 14 - b)
            n = count(lambda key, col: (key == thr) & (col < cand))
            return jnp.where(n < want, cand, a)

        cut = lax.fori_loop(0, 15, idx_step, jnp.zeros((Q, 1), I32)) + 1

        def rewrite(j, carry):
            off = pl.multiple_of(j * tk, tk)
            key = key_ref[:, pl.ds(off, tk)]
            col = col0 + off
            key = jnp.where((key == thr) & (col >= cut), thr - 1, key)
            key = jnp.where(col < lim, key, INT_MIN)
            key_ref[:, pl.ds(off, tk)] = key
            return carry

        lax.fori_loop(0, n_tiles, rewrite, 0)

    thr = jnp.maximum(thr, INT_MIN + 1)

    m_ref[...] = jnp.full(m_ref.shape, NEG_INF, F32)
    l_ref[...] = jnp.zeros(l_ref.shape, F32)
    acc_ref[...] = jnp.zeros(acc_ref.shape, F32)
    scale = HEAD_DIM ** -0.5

    def attn_tile(j, carry):
        off = pl.multiple_of(j * tk, tk)
        sel = key_ref[:, pl.ds(off, tk)] >= thr
        for hh in range(N_HEADS):
            lo, hi = hh * HEAD_DIM, (hh + 1) * HEAD_DIM
            s = lax.dot_general(qb_ref[0, :, lo:hi], k_ref[0, pl.ds(off, tk), lo:hi],
                                (((1,), (1,)), ((), ())), preferred_element_type=F32) * scale
            s = jnp.where(sel, s, NEG_INF)
            m_prev = m_ref[hh]
            m_new = jnp.maximum(m_prev, jnp.max(s, axis=1, keepdims=True))
            alpha = jnp.exp(m_prev - m_new)
            p = jnp.exp(s - m_new[:, 0:1])
            l_ref[hh] = alpha * l_ref[hh] + jnp.sum(p, axis=1, keepdims=True)
            pv = jnp.dot(p.astype(BF16), v_ref[0, pl.ds(off, tk), lo:hi], preferred_element_type=F32)
            acc_ref[:, lo:hi] = alpha * acc_ref[:, lo:hi] + pv
            m_ref[hh] = m_new
        return carry

    lax.fori_loop(0, n_tiles, attn_tile, 0)

    for hh in range(N_HEADS):
        lo, hi = hh * HEAD_DIM, (hh + 1) * HEAD_DIM
        acc_ref[:, lo:hi] = acc_ref[:, lo:hi] / l_ref[hh]
    ya_in = (sza_ref[0] * acc_ref[...]).astype(BF16)
    y_a = jnp.dot(ya_in, waout_ref[...], preferred_element_type=F32)
    merged = ga_ref[0] * y_a + gyb_ref[0]
    out = jnp.dot(merged.astype(BF16), wout_ref[...], preferred_element_type=F32)
    ms = jnp.mean(out * out, axis=-1, keepdims=True)
    y_ref[0] = x_ref[0] + (out * lax.rsqrt(ms + EPS)) * npost_ref[...]


def _attn_call(qb, qib, wi, k_all, v_all, kib_all, sza, ga, gyb, x, w, *, q_rows, tk, past_len,
               n_keys, k_top):
    nb, t_all, d = x.shape
    n_key_pad = k_all.shape[1]
    assert t_all % q_rows == 0 and n_key_pad % tk == 0 and tk % LANES == 0
    nq = t_all // q_rows
    q_blk = lambda width: pl.BlockSpec((1, q_rows, width), lambda b, i: (b, i, 0))
    keys_blk = lambda width: _resident((1, n_key_pad, width), lambda b, i: (b, 0, 0))
    full = lambda a: _resident(a.shape, lambda b, i: (0,) * a.ndim)
    weights = (w["w_a_out"], w["w_out"], w["norm_post"])
    return pl.pallas_call(
        functools.partial(_attn_kernel, q_rows=q_rows, tk=tk, past_len=past_len, n_keys=n_keys,
                          k_top=k_top),
        grid=(nb, nq),
        in_specs=[q_blk(ATTN_WIDTH), q_blk(N_IDX_HEADS * LANES), q_blk(N_IDX_HEADS),
                  keys_blk(ATTN_WIDTH), keys_blk(ATTN_WIDTH), keys_blk(LANES),
                  q_blk(ATTN_WIDTH), q_blk(D_MODEL), q_blk(D_MODEL), q_blk(d)]
                 + [full(a) for a in weights],
        out_specs=q_blk(d),
        out_shape=jax.ShapeDtypeStruct((nb, t_all, d), F32),
        scratch_shapes=[pltpu.VMEM((q_rows, n_key_pad), I32),
                        pltpu.VMEM((q_rows, ATTN_WIDTH), F32),
                        pltpu.VMEM((N_HEADS, q_rows, LANES), F32),
                        pltpu.VMEM((N_HEADS, q_rows, LANES), F32)],
        compiler_params=pltpu.CompilerParams(
            dimension_semantics=("arbitrary", "arbitrary"), vmem_limit_bytes=VMEM_LIMIT_BYTES),
        name="attn",
    )(qb, qib, wi, k_all, v_all, kib_all, sza, ga, gyb, x, *weights)


def _prep_weights(norm_pre, w_in, conv_w, w_a_out, w_b_out, b_merge, w_out, norm_post):
    W, C, D = ATTN_WIDTH, CONV_WIDTH, D_MODEL
    o_qi = 4 * W
    o_ki = o_qi + N_IDX_HEADS * IDX_DIM
    o_wi = o_ki + IDX_DIM
    o_conv = o_wi + N_IDX_HEADS
    o_g = o_conv + 4 * C
    bf = lambda a: a.astype(BF16)
    pad_lanes = lambda a: jnp.pad(a, ((0, 0), (0, LANES - a.shape[1])))
    w_qi = w_in[:, o_qi:o_ki].reshape(D, N_IDX_HEADS, IDX_DIM)
    w_qi = jnp.pad(w_qi, ((0, 0), (0, 0), (0, LANES - IDX_DIM))).reshape(D, N_IDX_HEADS * LANES)
    return {
        "norm_pre": norm_pre.reshape(1, D),
        "w_qkvz": bf(w_in[:, :o_qi]),
        "w_qi": bf(w_qi),
        "w_ki": bf(pad_lanes(w_in[:, o_ki:o_wi])),
        "w_wi": bf(pad_lanes(w_in[:, o_wi:o_conv])),
        "w_conv": bf(w_in[:, o_conv:o_g]),
        "w_g": bf(w_in[:, o_g:]),
        "conv_w": conv_w,
        "w_b_out": bf(w_b_out),
        "b_merge": b_merge,
        "w_a_out": bf(w_a_out),
        "w_out": bf(w_out),
        "norm_post": norm_post.reshape(1, D),
    }


def _round_up(a, b):
    return (a + b - 1) // b * b


def _hybrid_layer(x, k_past, v_past, ki_past, conv_past, w, *, flatten_streams):
    B, T, D = x.shape
    P = 0 if k_past is None else k_past.shape[1]
    n_keys = P + T
    k_top = min(TOPK_MAX, n_keys // 4)

    if flatten_streams:
        outs = _proj_call(x.reshape(1, B * T, D), conv_past, w, groups=B, rows=T, n_row_tiles=1)
    else:
        rows = min(T, 256)
        outs = _proj_call(x, conv_past, w, groups=1, rows=rows, n_row_tiles=T // rows)
    k, v, ki, cstate, qb, kb, vb, qib, kib, wi, sza, ga, gyb = outs
    per_stream = lambda a: a.reshape(B, T, a.shape[-1])
    k, v, ki, qb, kb, vb, qib, kib, wi, sza, ga, gyb = map(
        per_stream, (k, v, ki, qb, kb, vb, qib, kib, wi, sza, ga, gyb))

    q_rows = min(T, 128)
    tk = 512 if n_keys % 512 == 0 else 3 * LANES
    n_key_pad = _round_up(n_keys, tk)
    if P:
        k_all = jnp.concatenate([k_past.reshape(B, P, ATTN_WIDTH).astype(BF16), kb], axis=1)
        v_all = jnp.concatenate([v_past.reshape(B, P, ATTN_WIDTH).astype(BF16), vb], axis=1)
        ki_pad = jnp.pad(ki_past, ((0, 0), (0, 0), (0, LANES - IDX_DIM))).astype(BF16)
        kib_all = jnp.concatenate([ki_pad, kib], axis=1)
    else:
        k_all, v_all, kib_all = kb, vb, kib
    if n_key_pad != n_keys:
        pad = ((0, 0), (0, n_key_pad - n_keys), (0, 0))
        k_all, v_all, kib_all = jnp.pad(k_all, pad), jnp.pad(v_all, pad), jnp.pad(kib_all, pad)

    y = _attn_call(qb, qib, wi, k_all, v_all, kib_all, sza, ga, gyb, x, w, q_rows=q_rows, tk=tk,
                   past_len=P, n_keys=n_keys, k_top=k_top)
    return (y, k.reshape(B, T, N_HEADS, HEAD_DIM), v.reshape(B, T, N_HEADS, HEAD_DIM), ki, cstate)


def kernel(x_prompt, x_sample, cache_k, cache_v, cache_kidx, state_conv, norm_pre, w_in, conv_w,
           w_a_out, w_b_out, b_merge, w_out, norm_post):
    B = x_prompt.shape[0]
    zero_conv = jnp.zeros((B, CONV_K - 1, CONV_WIDTH), x_prompt.dtype)
    hp, hs = x_prompt, x_sample
    per_layer = []
    for layer in range(w_in.shape[0]):
        w = _prep_weights(norm_pre[layer], w_in[layer], conv_w[layer], w_a_out[layer],
                          w_b_out[layer], b_merge[layer], w_out[layer], norm_post[layer])
        hp, k1, v1, ki1, c1 = _hybrid_layer(hp, None, None, None, zero_conv, w,
                                            flatten_streams=False)
        hs, k2, v2, ki2, c2 = _hybrid_layer(hs, cache_k[layer], cache_v[layer], cache_kidx[layer],
                                            state_conv[layer], w, flatten_streams=True)
        per_layer.append((k1, v1, ki1, c1, k2, v2, ki2, c2))
    return (hp, hs) + tuple(jnp.stack(leaf) for leaf in zip(*per_layer))
```

```python
import functools

import numpy as np
import jax
import jax.numpy as jnp
from jax import lax
from jax.experimental import pallas as pl
from jax.experimental.pallas import tpu as pltpu

D_MODEL = 1024
CHUNK = 64
N_HEADS = 8
HEAD_DIM = 128
ATTN_WIDTH = N_HEADS * HEAD_DIM
N_IDX_HEADS = 8
IDX_DIM = 64
TOPK_MAX = 256
CONV_WIDTH = 1024
CONV_K = 3
EPS = 1e-6
NEG_INF = -1e30

LANES = 128
SUBLANES = 8
VMEM_LIMIT_BYTES = 60 * 1024 * 1024

INT_MIN = -(2 ** 31)
KEY_NEG_INF = -int(np.float32(NEG_INF).view(np.int32) & 0x7FFFFFFF)
KEY_INADMISSIBLE = KEY_NEG_INF - 1

F32 = jnp.float32
BF16 = jnp.bfloat16
I32 = jnp.int32
NT_DIMS = (((1,), (1,)), ((), ()))


def _resident(shape, index_map):
    return pl.BlockSpec(shape, index_map, pipeline_mode=pl.Buffered(1))


def _proj_kernel(x_ref, past_ref, npre_ref, wqkvz_ref, wqi_ref, wki_ref, wwit_ref, wconv_ref,
                 wg_ref, convw_ref, wbout_ref, bmerge_ref,
                 k_ref, v_ref, ki_ref, cstate_ref, qb_ref, kb_ref, vtb_ref, qib_ref, kib_ref,
                 wit_ref, sza_ref, ga_ref, gyb_ref, carry_ref, *, groups, rows, n_row_tiles):
    t = pl.program_id(1)
    x = x_ref[0]
    ms = jnp.mean(x * x, axis=-1, keepdims=True)
    h = ((x * lax.rsqrt(ms + EPS)) * npre_ref[...]).astype(BF16)

    def proj(w_ref, lo, hi):
        return jnp.dot(h, w_ref[:, lo:hi], preferred_element_type=F32)

    W = ATTN_WIDTH
    q = proj(wqkvz_ref, 0, W)
    qb_ref[0] = q.astype(BF16)
    k = proj(wqkvz_ref, W, 2 * W)
    k_ref[0] = k
    kb_ref[0] = k.astype(BF16)
    v = proj(wqkvz_ref, 2 * W, 3 * W)
    v_ref[0] = v
    vtb_ref[0] = v.T.astype(BF16)
    z_a = proj(wqkvz_ref, 3 * W, 4 * W)
    sza_ref[0] = z_a * jax.nn.sigmoid(z_a)
    qib_ref[0] = proj(wqi_ref, 0, N_IDX_HEADS * LANES).astype(BF16)
    ki = proj(wki_ref, 0, LANES)
    ki_ref[0] = ki[:, :IDX_DIM]
    kib_ref[0] = ki.astype(BF16)
    wit = lax.dot_general(wwit_ref[...], h, NT_DIMS, preferred_element_type=F32)
    wit_ref[0] = wit[:N_IDX_HEADS]

    C = CONV_WIDTH
    u = proj(wconv_ref, 0, C)
    gate_c = proj(wconv_ref, 2 * C, 3 * C)
    cin = gate_c * u
    gate_b = proj(wconv_ref, C, 2 * C)
    z_b = proj(wconv_ref, 3 * C, 4 * C)
    pre = (z_b * jax.nn.sigmoid(z_b)) * gate_b

    @pl.when(t == 0)
    def _():
        carry_ref[...] = past_ref[...]

    w0 = convw_ref[0:1, :]
    w1 = convw_ref[1:2, :]
    w2 = convw_ref[2:3, :]
    row = lax.broadcasted_iota(I32, (rows, C), 0)
    convs = []
    for g in range(groups):
        cg = cin[g * rows:(g + 1) * rows]
        past0 = carry_ref[g, 0:1, :]
        past1 = carry_ref[g, 1:2, :]
        r1 = jnp.where(row == 0, past1, pltpu.roll(cg, 1, 0))
        r2 = jnp.where(row == 0, past0, jnp.where(row == 1, past1, pltpu.roll(cg, 2, 0)))
        convs.append(w0 * r2 + w1 * r1 + w2 * cg)
        carry_ref[g] = cg[rows - (CONV_K - 1):rows]
    conv = convs[0] if groups == 1 else jnp.concatenate(convs, axis=0)

    @pl.when(t == n_row_tiles - 1)
    def _():
        cstate_ref[...] = carry_ref[...]

    y_b = jnp.dot((pre * conv).astype(BF16), wbout_ref[...], preferred_element_type=F32)
    g_a = proj(wg_ref, 0, D_MODEL)
    ga_ref[0] = jax.nn.sigmoid(g_a + bmerge_ref[0:1, :])
    g_b = proj(wg_ref, D_MODEL, 2 * D_MODEL)
    gyb_ref[0] = jax.nn.sigmoid(g_b + bmerge_ref[1:2, :]) * y_b


def _proj_call(x, conv_past, w, *, groups, rows, n_row_tiles):
    nb, t_all, d = x.shape
    m = groups * rows
    assert t_all == n_row_tiles * m and (groups == 1 or n_row_tiles == 1)
    row_blk = lambda width: pl.BlockSpec((1, m, width), lambda b, t: (b, t, 0))
    col_blk = lambda height: pl.BlockSpec((1, height, m), lambda b, t: (b, 0, t))
    full = lambda a: _resident(a.shape, lambda b, t: (0,) * a.ndim)
    state_blk = pl.BlockSpec((groups, CONV_K - 1, CONV_WIDTH), lambda b, t: (b, 0, 0))
    f32_out = lambda width: jax.ShapeDtypeStruct((nb, t_all, width), F32)
    bf_out = lambda width: jax.ShapeDtypeStruct((nb, t_all, width), BF16)
    weights = (w["norm_pre"], w["w_qkvz"], w["w_qi"], w["w_ki"], w["w_wit"], w["w_conv"], w["w_g"],
               w["conv_w"], w["w_b_out"], w["b_merge"])
    out_shapes = (
        f32_out(ATTN_WIDTH), f32_out(ATTN_WIDTH), f32_out(IDX_DIM),
        jax.ShapeDtypeStruct(conv_past.shape, F32),
        bf_out(ATTN_WIDTH), bf_out(ATTN_WIDTH),
        jax.ShapeDtypeStruct((nb, ATTN_WIDTH, t_all), BF16),
        bf_out(N_IDX_HEADS * LANES), bf_out(LANES),
        jax.ShapeDtypeStruct((nb, N_IDX_HEADS, t_all), F32),
        f32_out(ATTN_WIDTH), f32_out(D_MODEL), f32_out(D_MODEL),
    )
    out_specs = (
        row_blk(ATTN_WIDTH), row_blk(ATTN_WIDTH), row_blk(IDX_DIM), state_blk,
        row_blk(ATTN_WIDTH), row_blk(ATTN_WIDTH), col_blk(ATTN_WIDTH),
        row_blk(N_IDX_HEADS * LANES), row_blk(LANES), col_blk(N_IDX_HEADS),
        row_blk(ATTN_WIDTH), row_blk(D_MODEL), row_blk(D_MODEL),
    )
    return pl.pallas_call(
        functools.partial(_proj_kernel, groups=groups, rows=rows, n_row_tiles=n_row_tiles),
        grid=(nb, n_row_tiles),
        in_specs=[row_blk(d), state_blk] + [full(a) for a in weights],
        out_specs=out_specs,
        out_shape=out_shapes,
        scratch_shapes=[pltpu.VMEM((groups, CONV_K - 1, CONV_WIDTH), F32)],
        compiler_params=pltpu.CompilerParams(
            dimension_semantics=("arbitrary", "arbitrary"), vmem_limit_bytes=VMEM_LIMIT_BYTES),
        name="proj",
    )(x, conv_past, *weights)


def _floor_avg(a, b):
    return (a >> 1) + (b >> 1) + (a & b & 1)


def _attn_kernel(qb_ref, qib_ref, wit_ref, k_ref, vt_ref, kib_ref, o_ref,
                 key_ref, grp_ref, acc_ref, m_ref, l_ref, *, q_cols, n_valid_q, tk, past_len,
                 n_keys, k_top):
    i = pl.program_id(1)
    Q = q_cols
    n_key_pad = key_ref.shape[0]
    n_groups = grp_ref.shape[0]
    lane = lax.broadcasted_iota(I32, (1, Q), 1)
    qpos = past_len + i * Q + lane
    lim = jnp.minimum((qpos // CHUNK + 1) * CHUNK, n_keys)
    last_lim = jnp.minimum(((past_len + i * Q + n_valid_q - 1) // CHUNK + 1) * CHUNK, n_keys)
    n_tiles = (last_lim + tk - 1) // tk
    row0 = lax.broadcasted_iota(I32, (tk, Q), 0)

    wsc = wit_ref[0] * (IDX_DIM ** -0.5 * N_IDX_HEADS ** -0.5)
    grp_ref[...] = jnp.full(grp_ref.shape, INT_MIN, I32)

    def score_tile(j, carry):
        off = pl.multiple_of(j * tk, tk)
        kt = kib_ref[0, pl.ds(off, tk), :]
        s = jnp.zeros((tk, Q), F32)
        for hh in range(N_IDX_HEADS):
            qh = qib_ref[0, :, hh * LANES:(hh + 1) * LANES]
            d = lax.dot_general(kt, qh, NT_DIMS, preferred_element_type=F32)
            s = s + jnp.maximum(d, 0.0) * wsc[hh:hh + 1, :]
        bits = pltpu.bitcast(s, I32)
        key = jnp.where(bits < 0, INT_MIN - bits, bits)
        key = jnp.where(key < KEY_NEG_INF, key - 1, key)
        row = row0 + off
        key = jnp.where(row < lim, key, KEY_INADMISSIBLE)
        if n_key_pad != n_keys:
            key = jnp.where(row < n_keys, key, INT_MIN)
        key_ref[pl.ds(off, tk), :] = key
        g = grp_ref[...]
        for c in range(tk // n_groups):
            g = jnp.maximum(g, key[c * n_groups:(c + 1) * n_groups])
        grp_ref[...] = g
        return carry

    lax.fori_loop(0, n_tiles, score_tile, 0)

    n_part = 4 * SUBLANES

    def count(pred):
        def body(j, c):
            off = pl.multiple_of(j * tk, tk)
            hit = jnp.where(pred(key_ref[pl.ds(off, tk), :], row0 + off), 1, 0)
            return c + jnp.sum(hit.reshape(tk // n_part, n_part, Q), axis=0)
        c = lax.fori_loop(0, n_tiles, body, jnp.zeros((n_part, Q), I32))
        return jnp.sum(c, axis=0, keepdims=True)

    def count_ge(cand):
        return count(lambda key, row: key >= cand)

    g = grp_ref[...]
    lo0 = jnp.min(g, axis=0, keepdims=True)
    hi0 = jnp.max(g, axis=0, keepdims=True) + 1

    def unresolved(state):
        lo, hi, _ = state
        return jnp.max(jnp.where(_floor_avg(lo, hi) != lo, 1, 0)) > 0

    def bisect(state):
        lo, hi, n_lo = state
        mid = _floor_avg(lo, hi)
        n = count_ge(mid)
        ge = n >= k_top
        hi = jnp.where(ge, jnp.where(n == k_top, mid + 1, hi), mid)
        return jnp.where(ge, mid, lo), hi, jnp.where(ge, n, n_lo)

    thr, _, n_ge = lax.while_loop(unresolved, bisect,
                                  (lo0, hi0, jnp.full((1, Q), n_key_pad, I32)))

    odd = ((n_ge > k_top) | (thr <= KEY_INADMISSIBLE)) & (lane < n_valid_q)
    need_fix = jnp.max(jnp.where(odd, 1, 0)) > 0

    @pl.when(need_fix)
    def _():
        want = k_top - count_ge(thr + 1)

        def idx_step(b, a):
            cand = a | lax.shift_left(jnp.int32(1), (n_key_pad - 1).bit_length() - 1 - b)
            n = count(lambda key, row: (key == thr) & (row < cand))
            return jnp.where(n < want, cand, a)

        cut = lax.fori_loop(0, (n_key_pad - 1).bit_length(), idx_step,
                            jnp.zeros((1, Q), I32)) + 1

        def rewrite(j, carry):
            off = pl.multiple_of(j * tk, tk)
            key = key_ref[pl.ds(off, tk), :]
            row = row0 + off
            key = jnp.where((key == thr) & (row >= cut), thr - 1, key)
            key = jnp.where(row < lim, key, INT_MIN)
            key_ref[pl.ds(off, tk), :] = key
            return carry

        lax.fori_loop(0, n_tiles, rewrite, 0)

    thr = jnp.maximum(thr, INT_MIN + 1)

    m_ref[...] = jnp.full(m_ref.shape, NEG_INF, F32)
    l_ref[...] = jnp.zeros(l_ref.shape, F32)
    acc_ref[...] = jnp.zeros(acc_ref.shape, F32)
    scale = HEAD_DIM ** -0.5

    def attn_tile(j, carry):
        off = pl.multiple_of(j * tk, tk)
        sel = key_ref[pl.ds(off, tk), :] >= thr
        for hh in range(N_HEADS):
            lo, hi = hh * HEAD_DIM, (hh + 1) * HEAD_DIM
            s = lax.dot_general(k_ref[0, pl.ds(off, tk), lo:hi], qb_ref[0, :, lo:hi], NT_DIMS,
                                preferred_element_type=F32) * scale
            s = jnp.where(sel, s, NEG_INF)
            m_prev = m_ref[hh:hh + 1, :]
            m_new = jnp.maximum(m_prev, jnp.max(s, axis=0, keepdims=True))
            alpha = jnp.exp(m_prev - m_new)
            p = jnp.exp(s - m_new)
            l_ref[hh:hh + 1, :] = alpha * l_ref[hh:hh + 1, :] + jnp.sum(p, axis=0, keepdims=True)
            pv = jnp.dot(vt_ref[0, lo:hi, pl.ds(off, tk)], p.astype(BF16),
                         preferred_element_type=F32)
            acc_ref[lo:hi, :] = alpha * acc_ref[lo:hi, :] + pv
            m_ref[hh:hh + 1, :] = m_new
        return carry

    lax.fori_loop(0, n_tiles, attn_tile, 0)

    for hh in range(N_HEADS):
        lo, hi = hh * HEAD_DIM, (hh + 1) * HEAD_DIM
        acc_ref[lo:hi, :] = acc_ref[lo:hi, :] / l_ref[hh:hh + 1, :]
    o_ref[0] = acc_ref[...].T


def _attn_call(qb, qib, wit, k_all, vt_all, kib_all, *, q_cols, n_valid_q, tk, past_len, n_keys,
               k_top):
    nb, t_all, width = qb.shape
    n_key_pad = k_all.shape[1]
    n_groups = 2 * LANES
    assert t_all % q_cols == 0 and n_key_pad % tk == 0 and tk % n_groups == 0 and n_groups >= k_top
    nq = t_all // q_cols
    q_blk = lambda w_: pl.BlockSpec((1, q_cols, w_), lambda b, i: (b, i, 0))
    return pl.pallas_call(
        functools.partial(_attn_kernel, q_cols=q_cols, n_valid_q=n_valid_q, tk=tk,
                          past_len=past_len, n_keys=n_keys, k_top=k_top),
        grid=(nb, nq),
        in_specs=[q_blk(width), q_blk(N_IDX_HEADS * LANES),
                  pl.BlockSpec((1, N_IDX_HEADS, q_cols), lambda b, i: (b, 0, i)),
                  _resident((1, n_key_pad, width), lambda b, i: (b, 0, 0)),
                  _resident((1, width, n_key_pad), lambda b, i: (b, 0, 0)),
                  _resident((1, n_key_pad, LANES), lambda b, i: (b, 0, 0))],
        out_specs=q_blk(width),
        out_shape=jax.ShapeDtypeStruct((nb, t_all, width), F32),
        scratch_shapes=[pltpu.VMEM((n_key_pad, q_cols), I32),
                        pltpu.VMEM((n_groups, q_cols), I32),
                        pltpu.VMEM((width, q_cols), F32),
                        pltpu.VMEM((N_HEADS, q_cols), F32),
                        pltpu.VMEM((N_HEADS, q_cols), F32)],
        compiler_params=pltpu.CompilerParams(
            dimension_semantics=("arbitrary", "arbitrary"), vmem_limit_bytes=VMEM_LIMIT_BYTES),
        name="attn",
    )(qb, qib, wit, k_all, vt_all, kib_all)


def _post_kernel(attn_ref, sza_ref, ga_ref, gyb_ref, x_ref, waout_ref, wout_ref, npost_ref, y_ref):
    ya_in = (sza_ref[...] * attn_ref[...]).astype(BF16)
    y_a = jnp.dot(ya_in, waout_ref[...], preferred_element_type=F32)
    merged = ga_ref[...] * y_a + gyb_ref[...]
    out = jnp.dot(merged.astype(BF16), wout_ref[...], preferred_element_type=F32)
    ms = jnp.mean(out * out, axis=-1, keepdims=True)
    y_ref[...] = x_ref[...] + (out * lax.rsqrt(ms + EPS)) * npost_ref[...]


def _post_call(attn, sza, ga, gyb, x, w, *, rows):
    n, d = x.shape
    assert n % rows == 0
    row_blk = pl.BlockSpec((rows, d), lambda t: (t, 0))
    full = lambda a: _resident(a.shape, lambda t: (0,) * a.ndim)
    weights = (w["w_a_out"], w["w_out"], w["norm_post"])
    return pl.pallas_call(
        _post_kernel,
        grid=(n // rows,),
        in_specs=[row_blk] * 5 + [full(a) for a in weights],
        out_specs=row_blk,
        out_shape=jax.ShapeDtypeStruct((n, d), F32),
        compiler_params=pltpu.CompilerParams(
            dimension_semantics=("arbitrary",), vmem_limit_bytes=VMEM_LIMIT_BYTES),
        name="post",
    )(attn, sza, ga, gyb, x, *weights)


def _prep_weights(norm_pre, w_in, conv_w, w_a_out, w_b_out, b_merge, w_out, norm_post):
    W, C, D = ATTN_WIDTH, CONV_WIDTH, D_MODEL
    o_qi = 4 * W
    o_ki = o_qi + N_IDX_HEADS * IDX_DIM
    o_wi = o_ki + IDX_DIM
    o_conv = o_wi + N_IDX_HEADS
    o_g = o_conv + 4 * C
    bf = lambda a: a.astype(BF16)
    w_qi = w_in[:, o_qi:o_ki].reshape(D, N_IDX_HEADS, IDX_DIM)
    w_qi = jnp.pad(w_qi, ((0, 0), (0, 0), (0, LANES - IDX_DIM))).reshape(D, N_IDX_HEADS * LANES)
    return {
        "norm_pre": norm_pre.reshape(1, D),
        "w_qkvz": bf(w_in[:, :o_qi]),
        "w_qi": bf(w_qi),
        "w_ki": bf(jnp.pad(w_in[:, o_ki:o_wi], ((0, 0), (0, LANES - IDX_DIM)))),
        "w_wit": bf(jnp.pad(w_in[:, o_wi:o_conv].T, ((0, LANES - N_IDX_HEADS), (0, 0)))),
        "w_conv": bf(w_in[:, o_conv:o_g]),
        "w_g": bf(w_in[:, o_g:]),
        "conv_w": conv_w,
        "w_b_out": bf(w_b_out),
        "b_merge": b_merge,
        "w_a_out": bf(w_a_out),
        "w_out": bf(w_out),
        "norm_post": norm_post.reshape(1, D),
    }


def _round_up(a, b):
    return (a + b - 1) // b * b


def _hybrid_layer(x, k_past, v_past, ki_past, conv_past, w, *, flatten_streams):
    B, T, D = x.shape
    P = 0 if k_past is None else k_past.shape[1]
    n_keys = P + T
    k_top = min(TOPK_MAX, n_keys // 4)

    if flatten_streams:
        outs = _proj_call(x.reshape(1, B * T, D), conv_past, w, groups=B, rows=T, n_row_tiles=1)
    else:
        rows = min(T, 256)
        outs = _proj_call(x, conv_past, w, groups=1, rows=rows, n_row_tiles=T // rows)
    k, v, ki, cstate, qb, kb, vtb, qib, kib, wit, sza, ga, gyb = outs
    per_stream = lambda a: a.reshape(B, T, a.shape[-1])
    k, v, ki, qb, kb, qib, kib = map(per_stream, (k, v, ki, qb, kb, qib, kib))
    if flatten_streams:
        per_stream_t = lambda a: a.reshape(a.shape[1], B, T).transpose(1, 0, 2)
        vtb, wit = per_stream_t(vtb), per_stream_t(wit)

    q_cols = min(_round_up(T, LANES), 2 * LANES)
    t_pad = _round_up(T, q_cols)
    tk = 512 if n_keys % 512 == 0 else 2 * LANES
    n_key_pad = _round_up(n_keys, tk)
    if P:
        k_all = jnp.concatenate([k_past.reshape(B, P, ATTN_WIDTH).astype(BF16), kb], axis=1)
        vt_past = v_past.reshape(B, P, ATTN_WIDTH).astype(BF16).transpose(0, 2, 1)
        vt_all = jnp.concatenate([vt_past, vtb], axis=2)
        ki_pad = jnp.pad(ki_past, ((0, 0), (0, 0), (0, LANES - IDX_DIM))).astype(BF16)
        kib_all = jnp.concatenate([ki_pad, kib], axis=1)
    else:
        k_all, vt_all, kib_all = kb, vtb, kib
    if n_key_pad != n_keys:
        pad_rows = ((0, 0), (0, n_key_pad - n_keys), (0, 0))
        k_all, kib_all = jnp.pad(k_all, pad_rows), jnp.pad(kib_all, pad_rows)
        vt_all = jnp.pad(vt_all, ((0, 0), (0, 0), (0, n_key_pad - n_keys)))
    if t_pad != T:
        pad_q = ((0, 0), (0, t_pad - T), (0, 0))
        qb, qib = jnp.pad(qb, pad_q), jnp.pad(qib, pad_q)
        wit = jnp.pad(wit, ((0, 0), (0, 0), (0, t_pad - T)))
    attn = _attn_call(qb, qib, wit, k_all, vt_all, kib_all, q_cols=q_cols,
                      n_valid_q=min(T, q_cols), tk=tk, past_len=P, n_keys=n_keys, k_top=k_top)
    attn = attn[:, :T]

    flat = lambda a: a.reshape(B * T, a.shape[-1])
    y = _post_call(flat(attn), flat(sza), flat(ga), flat(gyb), flat(x), w, rows=min(B * T, 512))
    return (y.reshape(B, T, D), k.reshape(B, T, N_HEADS, HEAD_DIM),
            v.reshape(B, T, N_HEADS, HEAD_DIM), ki, cstate)


def kernel(x_prompt, x_sample, cache_k, cache_v, cache_kidx, state_conv, norm_pre, w_in, conv_w,
           w_a_out, w_b_out, b_merge, w_out, norm_post):
    B = x_prompt.shape[0]
    zero_conv = jnp.zeros((B, CONV_K - 1, CONV_WIDTH), x_prompt.dtype)
    hp, hs = x_prompt, x_sample
    per_layer = []
    for layer in range(w_in.shape[0]):
        w = _prep_weights(norm_pre[layer], w_in[layer], conv_w[layer], w_a_out[layer],
                          w_b_out[layer], b_merge[layer], w_out[layer], norm_post[layer])
        hp, k1, v1, ki1, c1 = _hybrid_layer(hp, None, None, None, zero_conv, w,
                                            flatten_streams=False)
        hs, k2, v2, ki2, c2 = _hybrid_layer(hs, cache_k[layer], cache_v[layer], cache_kidx[layer],
                                            state_conv[layer], w, flatten_streams=True)
        per_layer.append((k1, v1, ki1, c1, k2, v2, ki2, c2))
    return (hp, hs) + tuple(jnp.stack(leaf) for leaf in zip(*per_layer))
```

```python
import functools

import numpy as np
import jax
import jax.numpy as jnp
from jax import lax
from jax.experimental import pallas as pl
from jax.experimental.pallas import tpu as pltpu

D_MODEL = 1024
CHUNK = 64
N_HEADS = 8
HEAD_DIM = 128
ATTN_WIDTH = N_HEADS * HEAD_DIM
N_IDX_HEADS = 8
IDX_DIM = 64
TOPK_MAX = 256
CONV_WIDTH = 1024
CONV_K = 3
EPS = 1e-6
NEG_INF = -1e30

LANES = 128
SUBLANES = 8
VMEM_LIMIT_BYTES = 60 * 1024 * 1024

INT_MIN = -(2 ** 31)
KEY_NEG_INF = -int(np.float32(NEG_INF).view(np.int32) & 0x7FFFFFFF)
KEY_INADMISSIBLE = KEY_NEG_INF - 1

F32 = jnp.float32
BF16 = jnp.bfloat16
I32 = jnp.int32
I16 = jnp.int16
NT_DIMS = (((1,), (1,)), ((), ()))
QK_SCALE_LOG2 = HEAD_DIM ** -0.5 * float(np.log2(np.e))


def _resident(shape, index_map):
    return pl.BlockSpec(shape, index_map, pipeline_mode=pl.Buffered(1))


def _proj_kernel(x_ref, past_ref, npre_ref, wqkvz_ref, wqi_ref, wki_ref, wwit_ref, wconv_ref,
                 wg_ref, convw_ref, wbout_ref, bmerge_ref,
                 k_ref, v_ref, ki_ref, cstate_ref, qb_ref, kb_ref, vtb_ref, qib_ref, kib_ref,
                 wit_ref, sza_ref, ga_ref, gyb_ref, carry_ref, *, groups, rows, n_row_tiles):
    t = pl.program_id(1)
    x = x_ref[0]
    ms = jnp.mean(x * x, axis=-1, keepdims=True)
    h = ((x * lax.rsqrt(ms + EPS)) * npre_ref[...]).astype(BF16)

    def proj(w_ref, lo, hi):
        return jnp.dot(h, w_ref[:, lo:hi], preferred_element_type=F32)

    W = ATTN_WIDTH
    q = proj(wqkvz_ref, 0, W)
    qb_ref[0] = (q * QK_SCALE_LOG2).astype(BF16)
    k = proj(wqkvz_ref, W, 2 * W)
    k_ref[0] = k
    kb_ref[0] = k.astype(BF16)
    v = proj(wqkvz_ref, 2 * W, 3 * W)
    v_ref[0] = v
    vtb_ref[0] = v.T.astype(BF16)
    z_a = proj(wqkvz_ref, 3 * W, 4 * W)
    sza_ref[0] = z_a * jax.nn.sigmoid(z_a)
    qib_ref[0] = proj(wqi_ref, 0, N_IDX_HEADS * LANES).astype(BF16)
    ki = proj(wki_ref, 0, LANES)
    ki_ref[0] = ki[:, :IDX_DIM]
    kib_ref[0] = ki.astype(BF16)
    wit = lax.dot_general(wwit_ref[...], h, NT_DIMS, preferred_element_type=F32)
    wit_ref[0] = wit[:N_IDX_HEADS]

    C = CONV_WIDTH
    u = proj(wconv_ref, 0, C)
    gate_c = proj(wconv_ref, 2 * C, 3 * C)
    cin = gate_c * u
    gate_b = proj(wconv_ref, C, 2 * C)
    z_b = proj(wconv_ref, 3 * C, 4 * C)
    pre = (z_b * jax.nn.sigmoid(z_b)) * gate_b

    @pl.when(t == 0)
    def _():
        carry_ref[...] = past_ref[...]

    w0 = convw_ref[0:1, :]
    w1 = convw_ref[1:2, :]
    w2 = convw_ref[2:3, :]
    row = lax.broadcasted_iota(I32, (rows, C), 0)
    convs = []
    for g in range(groups):
        cg = cin[g * rows:(g + 1) * rows]
        past0 = carry_ref[g, 0:1, :]
        past1 = carry_ref[g, 1:2, :]
        r1 = jnp.where(row == 0, past1, pltpu.roll(cg, 1, 0))
        r2 = jnp.where(row == 0, past0, jnp.where(row == 1, past1, pltpu.roll(cg, 2, 0)))
        convs.append(w0 * r2 + w1 * r1 + w2 * cg)
        carry_ref[g] = cg[rows - (CONV_K - 1):rows]
    conv = convs[0] if groups == 1 else jnp.concatenate(convs, axis=0)

    @pl.when(t == n_row_tiles - 1)
    def _():
        cstate_ref[...] = carry_ref[...]

    y_b = jnp.dot((pre * conv).astype(BF16), wbout_ref[...], preferred_element_type=F32)
    g_a = proj(wg_ref, 0, D_MODEL)
    ga_ref[0] = jax.nn.sigmoid(g_a + bmerge_ref[0:1, :])
    g_b = proj(wg_ref, D_MODEL, 2 * D_MODEL)
    gyb_ref[0] = jax.nn.sigmoid(g_b + bmerge_ref[1:2, :]) * y_b


def _proj_call(x, conv_past, w, *, groups, rows, n_row_tiles):
    nb, t_all, d = x.shape
    m = groups * rows
    assert t_all == n_row_tiles * m and (groups == 1 or n_row_tiles == 1)
    row_blk = lambda width: pl.BlockSpec((1, m, width), lambda b, t: (b, t, 0))
    col_blk = lambda height: pl.BlockSpec((1, height, m), lambda b, t: (b, 0, t))
    full = lambda a: _resident(a.shape, lambda b, t: (0,) * a.ndim)
    state_blk = pl.BlockSpec((groups, CONV_K - 1, CONV_WIDTH), lambda b, t: (b, 0, 0))
    f32_out = lambda width: jax.ShapeDtypeStruct((nb, t_all, width), F32)
    bf_out = lambda width: jax.ShapeDtypeStruct((nb, t_all, width), BF16)
    weights = (w["norm_pre"], w["w_qkvz"], w["w_qi"], w["w_ki"], w["w_wit"], w["w_conv"], w["w_g"],
               w["conv_w"], w["w_b_out"], w["b_merge"])
    out_shapes = (
        f32_out(ATTN_WIDTH), f32_out(ATTN_WIDTH), f32_out(IDX_DIM),
        jax.ShapeDtypeStruct(conv_past.shape, F32),
        bf_out(ATTN_WIDTH), bf_out(ATTN_WIDTH),
        jax.ShapeDtypeStruct((nb, ATTN_WIDTH, t_all), BF16),
        bf_out(N_IDX_HEADS * LANES), bf_out(LANES),
        jax.ShapeDtypeStruct((nb, N_IDX_HEADS, t_all), F32),
        f32_out(ATTN_WIDTH), f32_out(D_MODEL), f32_out(D_MODEL),
    )
    out_specs = (
        row_blk(ATTN_WIDTH), row_blk(ATTN_WIDTH), row_blk(IDX_DIM), state_blk,
        row_blk(ATTN_WIDTH), row_blk(ATTN_WIDTH), col_blk(ATTN_WIDTH),
        row_blk(N_IDX_HEADS * LANES), row_blk(LANES), col_blk(N_IDX_HEADS),
        row_blk(ATTN_WIDTH), row_blk(D_MODEL), row_blk(D_MODEL),
    )
    return pl.pallas_call(
        functools.partial(_proj_kernel, groups=groups, rows=rows, n_row_tiles=n_row_tiles),
        grid=(nb, n_row_tiles),
        in_specs=[row_blk(d), state_blk] + [full(a) for a in weights],
        out_specs=out_specs,
        out_shape=out_shapes,
        scratch_shapes=[pltpu.VMEM((groups, CONV_K - 1, CONV_WIDTH), F32)],
        compiler_params=pltpu.CompilerParams(
            dimension_semantics=("arbitrary", "arbitrary"), vmem_limit_bytes=VMEM_LIMIT_BYTES),
        name="proj",
    )(x, conv_past, *weights)


def _attn_kernel(qb_ref, qib_ref, wit_ref, k_ref, vt_ref, kib_ref, o_ref,
                 hi_ref, lo_ref, grp_ref, acc_ref, m_ref, l_ref, bias_ref, s_ref, *, q_cols,
                 n_valid_q, tk, past_len, n_keys, k_top):
    i = pl.program_id(1)
    Q = q_cols
    n_key_pad = hi_ref.shape[0]
    n_groups = grp_ref.shape[0]
    lane = lax.broadcasted_iota(I32, (1, Q), 1)
    qpos = past_len + i * Q + lane
    lim = jnp.minimum((qpos // CHUNK + 1) * CHUNK, n_keys)
    last_lim = jnp.minimum(((past_len + i * Q + n_valid_q - 1) // CHUNK + 1) * CHUNK, n_keys)
    n_tiles = (last_lim + tk - 1) // tk
    row0 = lax.broadcasted_iota(I32, (tk, Q), 0)

    wsc = wit_ref[0] * (IDX_DIM ** -0.5 * N_IDX_HEADS ** -0.5)
    grp_ref[...] = jnp.full(grp_ref.shape, INT_MIN, I32)

    def score_tile(j, carry):
        off = pl.multiple_of(j * tk, tk)
        kt = kib_ref[0, pl.ds(off, tk), :]
        s = jnp.zeros((tk, Q), F32)
        for hh in range(N_IDX_HEADS):
            qh = qib_ref[0, :, hh * LANES:(hh + 1) * LANES]
            d = lax.dot_general(kt, qh, NT_DIMS, preferred_element_type=F32)
            s = s + jnp.maximum(d, 0.0) * wsc[hh:hh + 1, :]
        bits = pltpu.bitcast(s, I32)
        key = jnp.where(bits < 0, INT_MIN - bits, bits)
        key = jnp.where(key < KEY_NEG_INF, key - 1, key)
        row = row0 + off
        key = jnp.where(row < lim, key, KEY_INADMISSIBLE)
        if n_key_pad != n_keys:
            key = jnp.where(row < n_keys, key, INT_MIN)
        hi_ref[pl.ds(off, tk), :] = (key >> 16).astype(I16)
        lo_ref[pl.ds(off, tk), :] = ((key & 0xFFFF) - 2 ** 15).astype(I16)
        g = grp_ref[...]
        for c in range(tk // n_groups):
            g = jnp.maximum(g, key[c * n_groups:(c + 1) * n_groups])
        grp_ref[...] = g
        return carry

    lax.fori_loop(0, n_tiles, score_tile, 0)

    n_part = 4 * SUBLANES

    def count16(plane_ref, cand):
        cand16 = cand.astype(I16)

        def body(j, c):
            off = pl.multiple_of(j * tk, tk)
            hit = jnp.where(plane_ref[pl.ds(off, tk), :] >= cand16, jnp.int16(1), jnp.int16(0))
            for g in range(tk // n_part):
                c = c + hit[g * n_part:(g + 1) * n_part]
            return c

        c = lax.fori_loop(0, n_tiles, body, jnp.zeros((n_part, Q), I16))
        return jnp.sum(c.astype(I32), axis=0, keepdims=True)

    def bisect(plane_ref, lo, hi, n_lo, n_hi, want, n_steps):
        def step(_, st):
            lo, hi, n_lo, n_hi = st
            mid = (lo + hi) >> 1
            n = count16(plane_ref, mid)
            ge = n >= want
            return (jnp.where(ge, mid, lo), jnp.where(ge, hi, mid),
                    jnp.where(ge, n, n_lo), jnp.where(ge, n_hi, n))
        return lax.fori_loop(0, n_steps, step, (lo, hi, n_lo, n_hi))

    unknown = jnp.full((1, Q), n_key_pad + 1, I32)
    zero = jnp.zeros((1, Q), I32)
    g = grp_ref[...]
    lo0 = jnp.min(g, axis=0, keepdims=True) >> 16
    hi0 = (jnp.max(g, axis=0, keepdims=True) >> 16) + 1
    n_steps = jnp.max(32 - lax.clz(hi0 - lo0 - 1))
    h_thr, _, _, n_gt_h = bisect(hi_ref, lo0, hi0, unknown, zero, k_top, n_steps)

    want_lo = k_top - n_gt_h
    h_thr16 = h_thr.astype(I16)

    def park(j, carry):
        off = pl.multiple_of(j * tk, tk)
        lo_ref[pl.ds(off, tk), :] = jnp.where(hi_ref[pl.ds(off, tk), :] == h_thr16,
                                              lo_ref[pl.ds(off, tk), :], jnp.int16(-2 ** 15))
        return carry

    lax.fori_loop(0, n_tiles, park, 0)
    l_thr, _, n_ge_l, _ = bisect(lo_ref, jnp.full((1, Q), -2 ** 15, I32),
                                 jnp.full((1, Q), 2 ** 15, I32), unknown, zero, want_lo, 16)
    thr = h_thr * 2 ** 16 + (l_thr + 2 ** 15)

    odd = ((n_ge_l > want_lo) | (thr <= KEY_INADMISSIBLE)) & (lane < n_valid_q)
    need_fix = jnp.max(jnp.where(odd, 1, 0)) > 0

    @pl.when(need_fix)
    def _():
        def key_tile(off):
            return (hi_ref[pl.ds(off, tk), :].astype(I32) * 2 ** 16
                    + (lo_ref[pl.ds(off, tk), :].astype(I32) + 2 ** 15))

        def count(pred):
            def body(j, c):
                off = pl.multiple_of(j * tk, tk)
                hit = jnp.where(pred(key_tile(off), row0 + off), 1, 0)
                return c + jnp.sum(hit.reshape(tk // n_part, n_part, Q), axis=0)
            c = lax.fori_loop(0, n_tiles, body, jnp.zeros((n_part, Q), I32))
            return jnp.sum(c, axis=0, keepdims=True)

        want = k_top - count(lambda key, row: key > thr)
        n_bits = (n_key_pad - 1).bit_length()

        def idx_step(b, a):
            cand = a | lax.shift_left(jnp.int32(1), n_bits - 1 - b)
            n = count(lambda key, row: (key == thr) & (row < cand))
            return jnp.where(n < want, cand, a)

        cut = lax.fori_loop(0, n_bits, idx_step, zero) + 1

        def rewrite(j, carry):
            off = pl.multiple_of(j * tk, tk)
            row = row0 + off
            drop = ((key_tile(off) == thr) & (row >= cut)) | (row >= lim)
            hi32 = hi_ref[pl.ds(off, tk), :].astype(I32)
            hi_ref[pl.ds(off, tk), :] = jnp.where(drop, -2 ** 15, hi32).astype(I16)
            return carry

        lax.fori_loop(0, n_tiles, rewrite, 0)

    l_thr16 = l_thr.astype(I16)

    m_ref[...] = jnp.full(m_ref.shape, NEG_INF, F32)
    l_ref[...] = jnp.zeros(l_ref.shape, F32)
    acc_ref[...] = jnp.zeros(acc_ref.shape, F32)
    keep, mask = jnp.asarray(0.0, BF16), jnp.asarray(NEG_INF, BF16)

    def attn_tile(j, carry):
        off = pl.multiple_of(j * tk, tk)
        hi16 = hi_ref[pl.ds(off, tk), :]
        lo16 = lo_ref[pl.ds(off, tk), :]
        bias16 = jnp.where(hi16 > h_thr16, keep,
                           jnp.where(hi16 == h_thr16, jnp.where(lo16 >= l_thr16, keep, mask), mask))
        bias_ref[...] = bias16.astype(F32)
        m_news = []
        for hh in range(N_HEADS):
            lo, hi = hh * HEAD_DIM, (hh + 1) * HEAD_DIM
            s = lax.dot_general(k_ref[0, pl.ds(off, tk), lo:hi], qb_ref[0, :, lo:hi], NT_DIMS,
                                preferred_element_type=F32) + bias_ref[...]
            s_ref[hh] = s
            m_news.append(jnp.maximum(m_ref[hh:hh + 1, :], jnp.max(s, axis=0, keepdims=True)))
        for hh in range(N_HEADS):
            lo, hi = hh * HEAD_DIM, (hh + 1) * HEAD_DIM
            m_new = m_news[hh]
            alpha = jnp.exp2(m_ref[hh:hh + 1, :] - m_new)
            p = jnp.exp2(s_ref[hh] - m_new)
            l_ref[hh:hh + 1, :] = alpha * l_ref[hh:hh + 1, :] + jnp.sum(p, axis=0, keepdims=True)
            pv = jnp.dot(vt_ref[0, lo:hi, pl.ds(off, tk)], p.astype(BF16),
                         preferred_element_type=F32)
            acc_ref[lo:hi, :] = alpha * acc_ref[lo:hi, :] + pv
            m_ref[hh:hh + 1, :] = m_new
        return carry

    lax.fori_loop(0, n_tiles, attn_tile, 0)

    for hh in range(N_HEADS):
        lo, hi = hh * HEAD_DIM, (hh + 1) * HEAD_DIM
        acc_ref[lo:hi, :] = acc_ref[lo:hi, :] / l_ref[hh:hh + 1, :]
    o_ref[0] = acc_ref[...].T


def _attn_call(qb, qib, wit, k_all, vt_all, kib_all, *, q_cols, n_valid_q, tk, past_len, n_keys,
               k_top):
    nb, t_all, width = qb.shape
    n_key_pad = k_all.shape[1]
    n_groups = 2 * LANES
    assert t_all % q_cols == 0 and n_key_pad % tk == 0 and tk % n_groups == 0 and n_groups >= k_top
    nq = t_all // q_cols
    q_blk = lambda w_: pl.BlockSpec((1, q_cols, w_), lambda b, i: (b, i, 0))
    return pl.pallas_call(
        functools.partial(_attn_kernel, q_cols=q_cols, n_valid_q=n_valid_q, tk=tk,
                          past_len=past_len, n_keys=n_keys, k_top=k_top),
        grid=(nb, nq),
        in_specs=[q_blk(width), q_blk(N_IDX_HEADS * LANES),
                  pl.BlockSpec((1, N_IDX_HEADS, q_cols), lambda b, i: (b, 0, i)),
                  _resident((1, n_key_pad, width), lambda b, i: (b, 0, 0)),
                  _resident((1, width, n_key_pad), lambda b, i: (b, 0, 0)),
                  _resident((1, n_key_pad, LANES), lambda b, i: (b, 0, 0))],
        out_specs=q_blk(width),
        out_shape=jax.ShapeDtypeStruct((nb, t_all, width), F32),
        scratch_shapes=[pltpu.VMEM((n_key_pad, q_cols), I16),
                        pltpu.VMEM((n_key_pad, q_cols), I16),
                        pltpu.VMEM((n_groups, q_cols), I32),
                        pltpu.VMEM((width, q_cols), F32),
                        pltpu.VMEM((N_HEADS, q_cols), F32),
                        pltpu.VMEM((N_HEADS, q_cols), F32),
                        pltpu.VMEM((tk, q_cols), F32),
                        pltpu.VMEM((N_HEADS, tk, q_cols), F32)],
        compiler_params=pltpu.CompilerParams(
            dimension_semantics=("arbitrary", "arbitrary"), vmem_limit_bytes=VMEM_LIMIT_BYTES),
        name="attn",
    )(qb, qib, wit, k_all, vt_all, kib_all)


def _post_kernel(attn_ref, sza_ref, ga_ref, gyb_ref, x_ref, waout_ref, wout_ref, npost_ref, y_ref):
    ya_in = (sza_ref[...] * attn_ref[...]).astype(BF16)
    y_a = jnp.dot(ya_in, waout_ref[...], preferred_element_type=F32)
    merged = ga_ref[...] * y_a + gyb_ref[...]
    out = jnp.dot(merged.astype(BF16), wout_ref[...], preferred_element_type=F32)
    ms = jnp.mean(out * out, axis=-1, keepdims=True)
    y_ref[...] = x_ref[...] + (out * lax.rsqrt(ms + EPS)) * npost_ref[...]


def _post_call(attn, sza, ga, gyb, x, w, *, rows):
    n, d = x.shape
    assert n % rows == 0
    row_blk = pl.BlockSpec((rows, d), lambda t: (t, 0))
    full = lambda a: _resident(a.shape, lambda t: (0,) * a.ndim)
    weights = (w["w_a_out"], w["w_out"], w["norm_post"])
    return pl.pallas_call(
        _post_kernel,
        grid=(n // rows,),
        in_specs=[row_blk] * 5 + [full(a) for a in weights],
        out_specs=row_blk,
        out_shape=jax.ShapeDtypeStruct((n, d), F32),
        compiler_params=pltpu.CompilerParams(
            dimension_semantics=("arbitrary",), vmem_limit_bytes=VMEM_LIMIT_BYTES),
        name="post",
    )(attn, sza, ga, gyb, x, *weights)


def _prep_weights(norm_pre, w_in, conv_w, w_a_out, w_b_out, b_merge, w_out, norm_post):
    W, C, D = ATTN_WIDTH, CONV_WIDTH, D_MODEL
    o_qi = 4 * W
    o_ki = o_qi + N_IDX_HEADS * IDX_DIM
    o_wi = o_ki + IDX_DIM
    o_conv = o_wi + N_IDX_HEADS
    o_g = o_conv + 4 * C
    bf = lambda a: a.astype(BF16)
    w_qi = w_in[:, o_qi:o_ki].reshape(D, N_IDX_HEADS, IDX_DIM)
    w_qi = jnp.pad(w_qi, ((0, 0), (0, 0), (0, LANES - IDX_DIM))).reshape(D, N_IDX_HEADS * LANES)
    return {
        "norm_pre": norm_pre.reshape(1, D),
        "w_qkvz": bf(w_in[:, :o_qi]),
        "w_qi": bf(w_qi),
        "w_ki": bf(jnp.pad(w_in[:, o_ki:o_wi], ((0, 0), (0, LANES - IDX_DIM)))),
        "w_wit": bf(jnp.pad(w_in[:, o_wi:o_conv].T, ((0, LANES - N_IDX_HEADS), (0, 0)))),
        "w_conv": bf(w_in[:, o_conv:o_g]),
        "w_g": bf(w_in[:, o_g:]),
        "conv_w": conv_w,
        "w_b_out": bf(w_b_out),
        "b_merge": b_merge,
        "w_a_out": bf(w_a_out),
        "w_out": bf(w_out),
        "norm_post": norm_post.reshape(1, D),
    }


def _round_up(a, b):
    return (a + b - 1) // b * b


def _hybrid_layer(x, k_past, v_past, ki_past, conv_past, w, *, flatten_streams):
    B, T, D = x.shape
    P = 0 if k_past is None else k_past.shape[1]
    n_keys = P + T
    k_top = min(TOPK_MAX, n_keys // 4)

    if flatten_streams:
        outs = _proj_call(x.reshape(1, B * T, D), conv_past, w, groups=B, rows=T, n_row_tiles=1)
    else:
        rows = min(T, 256)
        outs = _proj_call(x, conv_past, w, groups=1, rows=rows, n_row_tiles=T // rows)
    k, v, ki, cstate, qb, kb, vtb, qib, kib, wit, sza, ga, gyb = outs
    per_stream = lambda a: a.reshape(B, T, a.shape[-1])
    k, v, ki, qb, kb, qib, kib = map(per_stream, (k, v, ki, qb, kb, qib, kib))
    if flatten_streams:
        per_stream_t = lambda a: a.reshape(a.shape[1], B, T).transpose(1, 0, 2)
        vtb, wit = per_stream_t(vtb), per_stream_t(wit)

    q_cols = min(_round_up(T, LANES), 2 * LANES)
    t_pad = _round_up(T, q_cols)
    tk = 512 if n_keys % 512 == 0 else 2 * LANES
    n_key_pad = _round_up(n_keys, tk)
    if P:
        k_all = jnp.concatenate([k_past.reshape(B, P, ATTN_WIDTH).astype(BF16), kb], axis=1)
        vt_past = v_past.reshape(B, P, ATTN_WIDTH).astype(BF16).transpose(0, 2, 1)
        vt_all = jnp.concatenate([vt_past, vtb], axis=2)
        ki_pad = jnp.pad(ki_past, ((0, 0), (0, 0), (0, LANES - IDX_DIM))).astype(BF16)
        kib_all = jnp.concatenate([ki_pad, kib], axis=1)
    else:
        k_all, vt_all, kib_all = kb, vtb, kib
    if n_key_pad != n_keys:
        pad_rows = ((0, 0), (0, n_key_pad - n_keys), (0, 0))
        k_all, kib_all = jnp.pad(k_all, pad_rows), jnp.pad(kib_all, pad_rows)
        vt_all = jnp.pad(vt_all, ((0, 0), (0, 0), (0, n_key_pad - n_keys)))
    if t_pad != T:
        pad_q = ((0, 0), (0, t_pad - T), (0, 0))
        qb, qib = jnp.pad(qb, pad_q), jnp.pad(qib, pad_q)
        wit = jnp.pad(wit, ((0, 0), (0, 0), (0, t_pad - T)))
    attn = _attn_call(qb, qib, wit, k_all, vt_all, kib_all, q_cols=q_cols,
                      n_valid_q=min(T, q_cols), tk=tk, past_len=P, n_keys=n_keys, k_top=k_top)
    attn = attn[:, :T]

    flat = lambda a: a.reshape(B * T, a.shape[-1])
    y = _post_call(flat(attn), flat(sza), flat(ga), flat(gyb), flat(x), w, rows=min(B * T, 512))
    return (y.reshape(B, T, D), k.reshape(B, T, N_HEADS, HEAD_DIM),
            v.reshape(B, T, N_HEADS, HEAD_DIM), ki, cstate)


def kernel(x_prompt, x_sample, cache_k, cache_v, cache_kidx, state_conv, norm_pre, w_in, conv_w,
           w_a_out, w_b_out, b_merge, w_out, norm_post):
    B = x_prompt.shape[0]
    zero_conv = jnp.zeros((B, CONV_K - 1, CONV_WIDTH), x_prompt.dtype)
    hp, hs = x_prompt, x_sample
    per_layer = []
    for layer in range(w_in.shape[0]):
        w = _prep_weights(norm_pre[layer], w_in[layer], conv_w[layer], w_a_out[layer],
                          w_b_out[layer], b_merge[layer], w_out[layer], norm_post[layer])
        hp, k1, v1, ki1, c1 = _hybrid_layer(hp, None, None, None, zero_conv, w,
                                            flatten_streams=False)
        hs, k2, v2, ki2, c2 = _hybrid_layer(hs, cache_k[layer], cache_v[layer], cache_kidx[layer],
                                            state_conv[layer], w, flatten_streams=True)
        per_layer.append((k1, v1, ki1, c1, k2, v2, ki2, c2))
    return (hp, hs) + tuple(jnp.stack(leaf) for leaf in zip(*per_layer))
```

```python
import functools

import numpy as np
import jax
import jax.numpy as jnp
from jax import lax
from jax.experimental import pallas as pl
from jax.experimental.pallas import tpu as pltpu

D_MODEL = 1024
CHUNK = 64
N_HEADS = 8
HEAD_DIM = 128
ATTN_WIDTH = N_HEADS * HEAD_DIM
N_IDX_HEADS = 8
IDX_DIM = 64
TOPK_MAX = 256
CONV_WIDTH = 1024
CONV_K = 3
EPS = 1e-6
NEG_INF = -1e30

LANES = 128
SUBLANES = 8
VMEM_LIMIT_BYTES = 60 * 1024 * 1024

INT_MIN = -(2 ** 31)
KEY_INF = 0x7F800000
BRACKET_SLACK = 1 << 17

F32 = jnp.float32
BF16 = jnp.bfloat16
I32 = jnp.int32
NT_DIMS = (((1,), (1,)), ((), ()))
QK_SCALE_LOG2 = HEAD_DIM ** -0.5 * float(np.log2(np.e))
HEAD_LOOKAHEAD = 4


def _resident(shape, index_map):
    return pl.BlockSpec(shape, index_map, pipeline_mode=pl.Buffered(1))


def _proj_kernel(x_ref, past_ref, npre_ref, wqkvz_ref, wqi_ref, wki_ref, wwit_ref, wconv_ref,
                 wg_ref, convw_ref, wbout_ref, bmerge_ref,
                 k_ref, v_ref, ki_ref, cstate_ref, qb_ref, kb_ref, vtb_ref, qib_ref, kib_ref,
                 wit_ref, sza_ref, ga_ref, gyb_ref, carry_ref, *, groups, rows, n_row_tiles):
    t = pl.program_id(1)
    x = x_ref[0]
    ms = jnp.mean(x * x, axis=-1, keepdims=True)
    h = ((x * lax.rsqrt(ms + EPS)) * npre_ref[...]).astype(BF16)

    def proj(w_ref, lo, hi):
        return jnp.dot(h, w_ref[:, lo:hi], preferred_element_type=F32)

    W = ATTN_WIDTH
    q = proj(wqkvz_ref, 0, W)
    qb_ref[0] = (q * QK_SCALE_LOG2).astype(BF16)
    k = proj(wqkvz_ref, W, 2 * W)
    k_ref[0] = k
    kb_ref[0] = k.astype(BF16)
    v = proj(wqkvz_ref, 2 * W, 3 * W)
    v_ref[0] = v
    vtb_ref[0] = v.T.astype(BF16)
    z_a = proj(wqkvz_ref, 3 * W, 4 * W)
    sza_ref[0] = z_a * jax.nn.sigmoid(z_a)
    qib_ref[0] = proj(wqi_ref, 0, N_IDX_HEADS * LANES).astype(BF16)
    ki = proj(wki_ref, 0, LANES)
    ki_ref[0] = ki[:, :IDX_DIM]
    kib_ref[0] = ki.astype(BF16)
    wit = lax.dot_general(wwit_ref[...], h, NT_DIMS, preferred_element_type=F32)
    wit_ref[0] = wit[:N_IDX_HEADS]

    C = CONV_WIDTH
    u = proj(wconv_ref, 0, C)
    gate_c = proj(wconv_ref, 2 * C, 3 * C)
    cin = gate_c * u
    gate_b = proj(wconv_ref, C, 2 * C)
    z_b = proj(wconv_ref, 3 * C, 4 * C)
    pre = (z_b * jax.nn.sigmoid(z_b)) * gate_b

    @pl.when(t == 0)
    def _():
        carry_ref[...] = past_ref[...]

    w0 = convw_ref[0:1, :]
    w1 = convw_ref[1:2, :]
    w2 = convw_ref[2:3, :]
    row = lax.broadcasted_iota(I32, (rows, C), 0)
    convs = []
    for g in range(groups):
        cg = cin[g * rows:(g + 1) * rows]
        past0 = carry_ref[g, 0:1, :]
        past1 = carry_ref[g, 1:2, :]
        r1 = jnp.where(row == 0, past1, pltpu.roll(cg, 1, 0))
        r2 = jnp.where(row == 0, past0, jnp.where(row == 1, past1, pltpu.roll(cg, 2, 0)))
        convs.append(w0 * r2 + w1 * r1 + w2 * cg)
        carry_ref[g] = cg[rows - (CONV_K - 1):rows]
    conv = convs[0] if groups == 1 else jnp.concatenate(convs, axis=0)

    @pl.when(t == n_row_tiles - 1)
    def _():
        cstate_ref[...] = carry_ref[...]

    y_b = jnp.dot((pre * conv).astype(BF16), wbout_ref[...], preferred_element_type=F32)
    g_a = proj(wg_ref, 0, D_MODEL)
    ga_ref[0] = jax.nn.sigmoid(g_a + bmerge_ref[0:1, :])
    g_b = proj(wg_ref, D_MODEL, 2 * D_MODEL)
    gyb_ref[0] = jax.nn.sigmoid(g_b + bmerge_ref[1:2, :]) * y_b


def _proj_call(x, conv_past, w, *, groups, rows, n_row_tiles):
    nb, t_all, d = x.shape
    m = groups * rows
    assert t_all == n_row_tiles * m and (groups == 1 or n_row_tiles == 1)
    row_blk = lambda width: pl.BlockSpec((1, m, width), lambda b, t: (b, t, 0))
    col_blk = lambda height: pl.BlockSpec((1, height, m), lambda b, t: (b, 0, t))
    full = lambda a: _resident(a.shape, lambda b, t: (0,) * a.ndim)
    state_blk = pl.BlockSpec((groups, CONV_K - 1, CONV_WIDTH), lambda b, t: (b, 0, 0))
    f32_out = lambda width: jax.ShapeDtypeStruct((nb, t_all, width), F32)
    bf_out = lambda width: jax.ShapeDtypeStruct((nb, t_all, width), BF16)
    weights = (w["norm_pre"], w["w_qkvz"], w["w_qi"], w["w_ki"], w["w_wit"], w["w_conv"], w["w_g"],
               w["conv_w"], w["w_b_out"], w["b_merge"])
    out_shapes = (
        f32_out(ATTN_WIDTH), f32_out(ATTN_WIDTH), f32_out(IDX_DIM),
        jax.ShapeDtypeStruct(conv_past.shape, F32),
        bf_out(ATTN_WIDTH), bf_out(ATTN_WIDTH),
        jax.ShapeDtypeStruct((nb, ATTN_WIDTH, t_all), BF16),
        bf_out(N_IDX_HEADS * LANES), bf_out(LANES),
        jax.ShapeDtypeStruct((nb, N_IDX_HEADS, t_all), F32),
        f32_out(ATTN_WIDTH), f32_out(D_MODEL), f32_out(D_MODEL),
    )
    out_specs = (
        row_blk(ATTN_WIDTH), row_blk(ATTN_WIDTH), row_blk(IDX_DIM), state_blk,
        row_blk(ATTN_WIDTH), row_blk(ATTN_WIDTH), col_blk(ATTN_WIDTH),
        row_blk(N_IDX_HEADS * LANES), row_blk(LANES), col_blk(N_IDX_HEADS),
        row_blk(ATTN_WIDTH), row_blk(D_MODEL), row_blk(D_MODEL),
    )
    return pl.pallas_call(
        functools.partial(_proj_kernel, groups=groups, rows=rows, n_row_tiles=n_row_tiles),
        grid=(nb, n_row_tiles),
        in_specs=[row_blk(d), state_blk] + [full(a) for a in weights],
        out_specs=out_specs,
        out_shape=out_shapes,
        scratch_shapes=[pltpu.VMEM((groups, CONV_K - 1, CONV_WIDTH), F32)],
        compiler_params=pltpu.CompilerParams(
            dimension_semantics=("arbitrary", "arbitrary"), vmem_limit_bytes=VMEM_LIMIT_BYTES),
        name="proj",
    )(x, conv_past, *weights)


def _attn_kernel(*refs, q_cols, n_valid_q, tk, past_len, n_keys, k_top):
    if past_len:
        (qb_ref, qib_ref, wit_ref, k_ref, vt_ref, kib_ref, kp_ref, vp_ref, kip_ref, o_ref,
         *scratch) = refs
    else:
        qb_ref, qib_ref, wit_ref, k_ref, vt_ref, kib_ref, o_ref, *scratch = refs
    sc_ref, grp_ref, thr_ref, acc_ref, m_ref, l_ref, bias_ref, s_ref = scratch
    i = pl.program_id(1)
    Q = q_cols
    n_key_pad = sc_ref.shape[0]
    n_groups = grp_ref.shape[0]
    n_past_tiles = past_len // tk

    class NewKeys:
        def __init__(self, j):
            self.off = pl.multiple_of((j - n_past_tiles) * tk, tk)

        def idx(self):
            return kib_ref[0, pl.ds(self.off, tk), :IDX_DIM]

        def k(self, lo, hi):
            return k_ref[0, pl.ds(self.off, tk), lo:hi]

        def vt(self, lo, hi):
            return vt_ref[0, lo:hi, pl.ds(self.off, tk)]

    class PastKeys:
        def __init__(self, j):
            self.off = pl.multiple_of(j * tk, tk)

        def idx(self):
            return kip_ref[0, pl.ds(self.off, tk), :].astype(BF16)

        def k(self, lo, hi):
            return kp_ref[0, pl.ds(self.off, tk), lo:hi].astype(BF16)

        def vt(self, lo, hi):
            return vp_ref[0, pl.ds(self.off, tk), lo:hi].T.astype(BF16)

    def for_key_tiles(n_tiles, body):
        if n_past_tiles:
            lax.fori_loop(0, jnp.minimum(n_tiles, n_past_tiles),
                          lambda j, c: (body(j, PastKeys(j)), c)[1], 0)
        lax.fori_loop(n_past_tiles, n_tiles, lambda j, c: (body(j, NewKeys(j)), c)[1], 0)

    lane = lax.broadcasted_iota(I32, (1, Q), 1)
    qpos = past_len + i * Q + lane
    lim = jnp.minimum((qpos // CHUNK + 1) * CHUNK, n_keys)
    last_lim = jnp.minimum(((past_len + i * Q + n_valid_q - 1) // CHUNK + 1) * CHUNK, n_keys)
    n_tiles = (last_lim + tk - 1) // tk
    row0 = lax.broadcasted_iota(I32, (tk, Q), 0)

    wsc = wit_ref[0] * (IDX_DIM ** -0.5 * N_IDX_HEADS ** -0.5)
    grp_ref[...] = jnp.full(grp_ref.shape, -jnp.inf, F32)

    def score_tile(j, src):
        off = pl.multiple_of(j * tk, tk)
        kt = src.idx()
        s = jnp.zeros((tk, Q), F32)
        for hh in range(N_IDX_HEADS):
            qh = qib_ref[0, :, hh * LANES:hh * LANES + IDX_DIM]
            d = lax.dot_general(kt, qh, NT_DIMS, preferred_element_type=F32)
            s = s + jnp.maximum(d, 0.0) * wsc[hh:hh + 1, :]
        row = row0 + off
        s = jnp.where(row < lim, s, NEG_INF)
        if n_key_pad != n_keys:
            s = jnp.where(row < n_keys, s, -jnp.inf)
        sc_ref[pl.ds(off, tk), :] = s
        g = grp_ref[...]
        for c in range(tk // n_groups):
            g = jnp.maximum(g, s[c * n_groups:(c + 1) * n_groups])
        grp_ref[...] = g

    for_key_tiles(n_tiles, score_tile)

    n_part = 4 * SUBLANES

    def key_of(x):
        bits = pltpu.bitcast(x, I32)
        return jnp.where(bits < 0, INT_MIN - bits, bits)

    def float_of(key):
        return pltpu.bitcast(jnp.where(key < 0, INT_MIN - key, key), F32)

    def floor_avg(a, b):
        return (a >> 1) + (b >> 1) + (a & b & 1)

    def count(pred):
        def body(j, c):
            off = pl.multiple_of(j * tk, tk)
            hit = jnp.where(pred(sc_ref[pl.ds(off, tk), :], row0 + off), 1, 0)
            return c + jnp.sum(hit.reshape(tk // n_part, n_part, Q), axis=0)
        c = lax.fori_loop(0, n_tiles, body, jnp.zeros((n_part, Q), I32))
        return jnp.sum(c, axis=0, keepdims=True)

    def bisect(lo, hi, n_steps):
        def step(_, st):
            lo, hi, n_lo = st
            mid = floor_avg(lo, hi)
            t = float_of(mid)
            n = count(lambda s, row: s >= t)
            ge = n >= k_top
            return jnp.where(ge, mid, lo), jnp.where(ge, hi, mid), jnp.where(ge, n, n_lo)
        lo, _, n_lo = lax.fori_loop(0, n_steps, step, (lo, hi, unknown))
        return lo, n_lo

    unknown = jnp.full((1, Q), n_key_pad + 1, I32)
    zero = jnp.zeros((1, Q), I32)
    g = grp_ref[...]
    lo0 = jnp.maximum(key_of(jnp.min(g, axis=0, keepdims=True)) - BRACKET_SLACK, -KEY_INF)
    hi0 = jnp.minimum(key_of(jnp.max(g, axis=0, keepdims=True)) + BRACKET_SLACK, KEY_INF) + 1
    width = hi0 - lo0
    n_steps = jnp.max(jnp.where(width <= 0, 32, 32 - lax.clz(width - 1)))
    thr_key, n_ge = bisect(lo0, hi0, n_steps)
    thr = float_of(thr_key)
    thr_ref[...] = jnp.broadcast_to(thr, thr_ref.shape)

    odd = ((n_ge != k_top) | (thr <= NEG_INF)) & (lane < n_valid_q)
    need_fix = jnp.max(jnp.where(odd, 1, 0)) > 0

    @pl.when(need_fix)
    def _():
        def min_where(pred):
            def body(j, m):
                off = pl.multiple_of(j * tk, tk)
                s = sc_ref[pl.ds(off, tk), :]
                cand = jnp.where(pred(s), s, jnp.inf)
                return jnp.minimum(m, jnp.min(cand.reshape(tk // n_part, n_part, Q), axis=0))
            m = lax.fori_loop(0, n_tiles, body, jnp.full((n_part, Q), jnp.inf, F32))
            return jnp.min(m, axis=0, keepdims=True)

        full_lo = jnp.full((1, Q), -KEY_INF, I32)
        t_key, _ = bisect(full_lo, jnp.full((1, Q), KEY_INF + 1, I32), 32)
        t_grid = float_of(t_key)
        t0 = min_where(lambda s: s >= t_grid)

        def crowded(st):
            _, n_gt = st
            return jnp.max(jnp.where(n_gt >= k_top, 1, 0)) > 0

        def raise_thr(st):
            t, n_gt = st
            t = jnp.where(n_gt >= k_top, min_where(lambda s: s > t), t)
            return t, count(lambda s, row: s > t)

        t, n_gt = lax.while_loop(crowded, raise_thr, (t0, count(lambda s, row: s > t0)))

        want = k_top - n_gt
        n_bits = (n_key_pad - 1).bit_length()

        def idx_step(b, a):
            cand = a | lax.shift_left(jnp.int32(1), n_bits - 1 - b)
            n = count(lambda s, row: (s == t) & (row < cand))
            return jnp.where(n < want, cand, a)

        cut = lax.fori_loop(0, n_bits, idx_step, zero) + 1

        def rewrite(j, carry):
            off = pl.multiple_of(j * tk, tk)
            row = row0 + off
            s = sc_ref[pl.ds(off, tk), :]
            drop = ((s == t) & (row >= cut)) | (row >= lim)
            sc_ref[pl.ds(off, tk), :] = jnp.where(drop, -jnp.inf, s)
            return carry

        lax.fori_loop(0, n_tiles, rewrite, 0)
        thr_ref[...] = jnp.broadcast_to(t, thr_ref.shape)

    thr = thr_ref[0:1, :]

    m_ref[...] = jnp.full(m_ref.shape, NEG_INF, F32)
    l_ref[...] = jnp.zeros(l_ref.shape, F32)
    acc_ref[...] = jnp.zeros(acc_ref.shape, F32)
    ones_rows = jnp.ones((2 * SUBLANES, tk), BF16)

    def attn_tile(j, src):
        off = pl.multiple_of(j * tk, tk)
        bias_ref[...] = jnp.where(sc_ref[pl.ds(off, tk), :] >= thr, 0.0, NEG_INF)

        def logits(hh):
            lo, hi = hh * HEAD_DIM, (hh + 1) * HEAD_DIM
            s = lax.dot_general(src.k(lo, hi), qb_ref[0, :, lo:hi], NT_DIMS,
                                preferred_element_type=F32) + bias_ref[...]
            s_ref[hh] = s
            return jnp.maximum(m_ref[hh], jnp.max(s, axis=0, keepdims=True))

        def accumulate(hh, m_new):
            lo, hi = hh * HEAD_DIM, (hh + 1) * HEAD_DIM
            alpha = jnp.exp2(m_ref[hh] - m_new)
            p = jnp.exp2(s_ref[hh] - m_new[0:1])
            vt1 = jnp.concatenate([src.vt(lo, hi), ones_rows], axis=0)
            pv = jnp.dot(vt1, p.astype(BF16), preferred_element_type=F32)
            l_ref[hh] = alpha * l_ref[hh] + pv[HEAD_DIM:HEAD_DIM + SUBLANES]
            acc_ref[lo:hi, :] = alpha[0:1] * acc_ref[lo:hi, :] + pv[:HEAD_DIM]
            m_ref[hh] = m_new

        m_news = [logits(hh) for hh in range(HEAD_LOOKAHEAD)]
        for hh in range(N_HEADS):
            if hh + HEAD_LOOKAHEAD < N_HEADS:
                m_news.append(logits(hh + HEAD_LOOKAHEAD))
            accumulate(hh, m_news[hh])

    for_key_tiles(n_tiles, attn_tile)

    for hh in range(N_HEADS):
        lo, hi = hh * HEAD_DIM, (hh + 1) * HEAD_DIM
        acc_ref[lo:hi, :] = acc_ref[lo:hi, :] / l_ref[hh, 0:1, :]
    o_ref[0] = acc_ref[...].T


def _attn_call(qb, qib, wit, k_new, vt_new, kib_new, past, past_base, *, q_cols, n_valid_q, tk,
               n_keys, k_top):
    nb, t_all, width = qb.shape
    past_len = past[0].shape[1] if past else 0
    n_new_pad = k_new.shape[1]
    n_key_pad = past_len + n_new_pad
    n_groups = 2 * LANES
    assert t_all % q_cols == 0 and n_new_pad % tk == 0 and past_len % tk == 0
    assert tk % n_groups == 0 and n_groups >= k_top and n_key_pad >= n_keys
    nq = t_all // q_cols
    q_blk = lambda w_: pl.BlockSpec((1, q_cols, w_), lambda b, i: (b, i, 0))
    keys_blk = _resident if nq > 1 else pl.BlockSpec
    stream_blk = lambda a: keys_blk((1,) + a.shape[1:], lambda b, i: (b, 0, 0))
    return pl.pallas_call(
        functools.partial(_attn_kernel, q_cols=q_cols, n_valid_q=n_valid_q, tk=tk,
                          past_len=past_len, n_keys=n_keys, k_top=k_top),
        grid=(nb, nq),
        in_specs=[q_blk(width), q_blk(N_IDX_HEADS * LANES),
                  pl.BlockSpec((1, N_IDX_HEADS, q_cols), lambda b, i: (b, 0, i)),
                  stream_blk(k_new), stream_blk(vt_new), stream_blk(kib_new)]
                 + [pl.BlockSpec((1,) + a.shape[1:], lambda b, i: (past_base + b, 0, 0))
                    for a in past],
        out_specs=q_blk(width),
        out_shape=jax.ShapeDtypeStruct((nb, t_all, width), F32),
        scratch_shapes=[pltpu.VMEM((n_key_pad, q_cols), F32),
                        pltpu.VMEM((n_groups, q_cols), F32),
                        pltpu.VMEM((SUBLANES, q_cols), F32),
                        pltpu.VMEM((width, q_cols), F32),
                        pltpu.VMEM((N_HEADS, SUBLANES, q_cols), F32),
                        pltpu.VMEM((N_HEADS, SUBLANES, q_cols), F32),
                        pltpu.VMEM((tk, q_cols), F32),
                        pltpu.VMEM((N_HEADS, tk, q_cols), F32)],
        compiler_params=pltpu.CompilerParams(
            dimension_semantics=("arbitrary", "arbitrary"), vmem_limit_bytes=VMEM_LIMIT_BYTES),
        name="attn",
    )(qb, qib, wit, k_new, vt_new, kib_new, *past)


def _post_kernel(attn_ref, sza_ref, ga_ref, gyb_ref, x_ref, waout_ref, wout_ref, npost_ref, y_ref):
    ya_in = (sza_ref[...] * attn_ref[...]).astype(BF16)
    y_a = jnp.dot(ya_in, waout_ref[...], preferred_element_type=F32)
    merged = ga_ref[...] * y_a + gyb_ref[...]
    out = jnp.dot(merged.astype(BF16), wout_ref[...], preferred_element_type=F32)
    ms = jnp.mean(out * out, axis=-1, keepdims=True)
    y_ref[...] = x_ref[...] + (out * lax.rsqrt(ms + EPS)) * npost_ref[...]


def _post_call(attn, sza, ga, gyb, x, w, *, rows):
    n, d = x.shape
    assert n % rows == 0
    row_blk = pl.BlockSpec((rows, d), lambda t: (t, 0))
    full = lambda a: _resident(a.shape, lambda t: (0,) * a.ndim)
    weights = (w["w_a_out"], w["w_out"], w["norm_post"])
    return pl.pallas_call(
        _post_kernel,
        grid=(n // rows,),
        in_specs=[row_blk] * 5 + [full(a) for a in weights],
        out_specs=row_blk,
        out_shape=jax.ShapeDtypeStruct((n, d), F32),
        compiler_params=pltpu.CompilerParams(
            dimension_semantics=("arbitrary",), vmem_limit_bytes=VMEM_LIMIT_BYTES),
        name="post",
    )(attn, sza, ga, gyb, x, *weights)


def _prep_weights(norm_pre, w_in, conv_w, w_a_out, w_b_out, b_merge, w_out, norm_post):
    W, C, D = ATTN_WIDTH, CONV_WIDTH, D_MODEL
    o_qi = 4 * W
    o_ki = o_qi + N_IDX_HEADS * IDX_DIM
    o_wi = o_ki + IDX_DIM
    o_conv = o_wi + N_IDX_HEADS
    o_g = o_conv + 4 * C
    bf = lambda a: a.astype(BF16)
    w_qi = w_in[:, o_qi:o_ki].reshape(D, N_IDX_HEADS, IDX_DIM)
    w_qi = jnp.pad(w_qi, ((0, 0), (0, 0), (0, LANES - IDX_DIM))).reshape(D, N_IDX_HEADS * LANES)
    return {
        "norm_pre": norm_pre.reshape(1, D),
        "w_qkvz": bf(w_in[:, :o_qi]),
        "w_qi": bf(w_qi),
        "w_ki": bf(jnp.pad(w_in[:, o_ki:o_wi], ((0, 0), (0, LANES - IDX_DIM)))),
        "w_wit": bf(jnp.pad(w_in[:, o_wi:o_conv].T, ((0, LANES - N_IDX_HEADS), (0, 0)))),
        "w_conv": bf(w_in[:, o_conv:o_g]),
        "w_g": bf(w_in[:, o_g:]),
        "conv_w": conv_w,
        "w_b_out": bf(w_b_out),
        "b_merge": b_merge,
        "w_a_out": bf(w_a_out),
        "w_out": bf(w_out),
        "norm_post": norm_post.reshape(1, D),
    }


def _round_up(a, b):
    return (a + b - 1) // b * b


def _hybrid_layer(x, past, past_base, conv_past, w, *, flatten_streams):
    B, T, D = x.shape
    P = past[0].shape[1] if past else 0
    n_keys = P + T
    k_top = min(TOPK_MAX, n_keys // 4)

    if flatten_streams:
        outs = _proj_call(x.reshape(1, B * T, D), conv_past, w, groups=B, rows=T, n_row_tiles=1)
    else:
        rows = min(T, 256)
        outs = _proj_call(x, conv_past, w, groups=1, rows=rows, n_row_tiles=T // rows)
    k, v, ki, cstate, qb, kb, vtb, qib, kib, wit, sza, ga, gyb = outs
    per_stream = lambda a: a.reshape(B, T, a.shape[-1])
    k, v, ki, qb, kb, qib, kib = map(per_stream, (k, v, ki, qb, kb, qib, kib))
    if flatten_streams:
        per_stream_t = lambda a: a.reshape(a.shape[1], B, T).transpose(1, 0, 2)
        vtb, wit = per_stream_t(vtb), per_stream_t(wit)

    q_cols = min(_round_up(T, LANES), 2 * LANES)
    t_pad = _round_up(T, q_cols)
    tk = 512 if T % 512 == 0 and P % 512 == 0 else 2 * LANES
    n_new_pad = _round_up(T, tk)
    if n_new_pad != T:
        pad_rows = ((0, 0), (0, n_new_pad - T), (0, 0))
        kb, kib = jnp.pad(kb, pad_rows), jnp.pad(kib, pad_rows)
        vtb = jnp.pad(vtb, ((0, 0), (0, 0), (0, n_new_pad - T)))
    if t_pad != T:
        pad_q = ((0, 0), (0, t_pad - T), (0, 0))
        qb, qib = jnp.pad(qb, pad_q), jnp.pad(qib, pad_q)
        wit = jnp.pad(wit, ((0, 0), (0, 0), (0, t_pad - T)))
    attn = _attn_call(qb, qib, wit, kb, vtb, kib, past, past_base, q_cols=q_cols,
                      n_valid_q=min(T, q_cols), tk=tk, n_keys=n_keys, k_top=k_top)
    attn = attn[:, :T]

    flat = lambda a: a.reshape(B * T, a.shape[-1])
    y = _post_call(flat(attn), flat(sza), flat(ga), flat(gyb), flat(x), w, rows=min(B * T, 512))
    return (y.reshape(B, T, D), k.reshape(B, T, N_HEADS, HEAD_DIM),
            v.reshape(B, T, N_HEADS, HEAD_DIM), ki, cstate)


def kernel(x_prompt, x_sample, cache_k, cache_v, cache_kidx, state_conv, norm_pre, w_in, conv_w,
           w_a_out, w_b_out, b_merge, w_out, norm_post):
    B = x_prompt.shape[0]
    zero_conv = jnp.zeros((B, CONV_K - 1, CONV_WIDTH), x_prompt.dtype)
    hp, hs = x_prompt, x_sample
    n_run, past_len = cache_k.shape[1], cache_k.shape[2]
    caches = (cache_k.reshape(-1, past_len, ATTN_WIDTH), cache_v.reshape(-1, past_len, ATTN_WIDTH),
              cache_kidx.reshape(-1, past_len, IDX_DIM))
    per_layer = []
    for layer in range(w_in.shape[0]):
        w = _prep_weights(norm_pre[layer], w_in[layer], conv_w[layer], w_a_out[layer],
                          w_b_out[layer], b_merge[layer], w_out[layer], norm_post[layer])
        hp, k1, v1, ki1, c1 = _hybrid_layer(hp, (), 0, zero_conv, w, flatten_streams=False)
        hs, k2, v2, ki2, c2 = _hybrid_layer(hs, caches, layer * n_run, state_conv[layer], w,
                                            flatten_streams=True)
        per_layer.append((k1, v1, ki1, c1, k2, v2, ki2, c2))
    return (hp, hs) + tuple(jnp.stack(leaf) for leaf in zip(*per_layer))
```

```python
import functools

import numpy as np
import jax
import jax.numpy as jnp
from jax import lax
from jax.experimental import pallas as pl
from jax.experimental.pallas import tpu as pltpu

D_MODEL = 1024
CHUNK = 64
N_HEADS = 8
HEAD_DIM = 128
ATTN_WIDTH = N_HEADS * HEAD_DIM
N_IDX_HEADS = 8
IDX_DIM = 64
TOPK_MAX = 256
CONV_WIDTH = 1024
CONV_K = 3
EPS = 1e-6
NEG_INF = -1e30

LANES = 128
SUBLANES = 8
VMEM_LIMIT_BYTES = 60 * 1024 * 1024

INT_MIN = -(2 ** 31)
KEY_INF = 0x7F800000

F32 = jnp.float32
BF16 = jnp.bfloat16
I32 = jnp.int32
I16 = jnp.int16
NT_DIMS = (((1,), (1,)), ((), ()))
QK_SCALE_LOG2 = HEAD_DIM ** -0.5 * float(np.log2(np.e))
HEAD_LOOKAHEAD = 4


def _resident(shape, index_map):
    return pl.BlockSpec(shape, index_map, pipeline_mode=pl.Buffered(1))


def _proj_kernel(x_ref, past_ref, npre_ref, wqkvz_ref, wqi_ref, wki_ref, wwit_ref, wconv_ref,
                 wg_ref, convw_ref, wbout_ref, bmerge_ref,
                 k_ref, v_ref, ki_ref, cstate_ref, qb_ref, kb_ref, vtb_ref, qib_ref, kib_ref,
                 wit_ref, sza_ref, ga_ref, gyb_ref, carry_ref, *, groups, rows, n_row_tiles):
    t = pl.program_id(1)
    x = x_ref[0]
    ms = jnp.mean(x * x, axis=-1, keepdims=True)
    h = ((x * lax.rsqrt(ms + EPS)) * npre_ref[...]).astype(BF16)

    def proj(w_ref, lo, hi):
        return jnp.dot(h, w_ref[:, lo:hi], preferred_element_type=F32)

    W = ATTN_WIDTH
    q = proj(wqkvz_ref, 0, W)
    qb_ref[0] = (q * QK_SCALE_LOG2).astype(BF16)
    k = proj(wqkvz_ref, W, 2 * W)
    k_ref[0] = k
    kb_ref[0] = k.astype(BF16)
    v = proj(wqkvz_ref, 2 * W, 3 * W)
    v_ref[0] = v
    vtb_ref[0] = v.T.astype(BF16)
    z_a = proj(wqkvz_ref, 3 * W, 4 * W)
    sza_ref[0] = z_a * jax.nn.sigmoid(z_a)
    qib_ref[0] = proj(wqi_ref, 0, N_IDX_HEADS * LANES).astype(BF16)
    ki = proj(wki_ref, 0, LANES)
    ki_ref[0] = ki[:, :IDX_DIM]
    kib_ref[0] = ki.astype(BF16)
    wit = lax.dot_general(wwit_ref[...], h, NT_DIMS, preferred_element_type=F32)
    wit_ref[0] = wit[:N_IDX_HEADS]

    C = CONV_WIDTH
    u = proj(wconv_ref, 0, C)
    gate_c = proj(wconv_ref, 2 * C, 3 * C)
    cin = gate_c * u
    gate_b = proj(wconv_ref, C, 2 * C)
    z_b = proj(wconv_ref, 3 * C, 4 * C)
    pre = (z_b * jax.nn.sigmoid(z_b)) * gate_b

    @pl.when(t == 0)
    def _():
        carry_ref[...] = past_ref[...]

    w0 = convw_ref[0:1, :]
    w1 = convw_ref[1:2, :]
    w2 = convw_ref[2:3, :]
    row = lax.broadcasted_iota(I32, (rows, C), 0)
    convs = []
    for g in range(groups):
        cg = cin[g * rows:(g + 1) * rows]
        past0 = carry_ref[g, 0:1, :]
        past1 = carry_ref[g, 1:2, :]
        r1 = jnp.where(row == 0, past1, pltpu.roll(cg, 1, 0))
        r2 = jnp.where(row == 0, past0, jnp.where(row == 1, past1, pltpu.roll(cg, 2, 0)))
        convs.append(w0 * r2 + w1 * r1 + w2 * cg)
        carry_ref[g] = cg[rows - (CONV_K - 1):rows]
    conv = convs[0] if groups == 1 else jnp.concatenate(convs, axis=0)

    @pl.when(t == n_row_tiles - 1)
    def _():
        cstate_ref[...] = carry_ref[...]

    y_b = jnp.dot((pre * conv).astype(BF16), wbout_ref[...], preferred_element_type=F32)
    g_a = proj(wg_ref, 0, D_MODEL)
    ga_ref[0] = jax.nn.sigmoid(g_a + bmerge_ref[0:1, :])
    g_b = proj(wg_ref, D_MODEL, 2 * D_MODEL)
    gyb_ref[0] = jax.nn.sigmoid(g_b + bmerge_ref[1:2, :]) * y_b


def _proj_call(x, conv_past, w, *, groups, rows, n_row_tiles):
    nb, t_all, d = x.shape
    m = groups * rows
    assert t_all == n_row_tiles * m and (groups == 1 or n_row_tiles == 1)
    row_blk = lambda width: pl.BlockSpec((1, m, width), lambda b, t: (b, t, 0))
    col_blk = lambda height: pl.BlockSpec((1, height, m), lambda b, t: (b, 0, t))
    full = lambda a: _resident(a.shape, lambda b, t: (0,) * a.ndim)
    state_blk = pl.BlockSpec((groups, CONV_K - 1, CONV_WIDTH), lambda b, t: (b, 0, 0))
    f32_out = lambda width: jax.ShapeDtypeStruct((nb, t_all, width), F32)
    bf_out = lambda width: jax.ShapeDtypeStruct((nb, t_all, width), BF16)
    weights = (w["norm_pre"], w["w_qkvz"], w["w_qi"], w["w_ki"], w["w_wit"], w["w_conv"], w["w_g"],
               w["conv_w"], w["w_b_out"], w["b_merge"])
    out_shapes = (
        f32_out(ATTN_WIDTH), f32_out(ATTN_WIDTH), f32_out(IDX_DIM),
        jax.ShapeDtypeStruct(conv_past.shape, F32),
        bf_out(ATTN_WIDTH), bf_out(ATTN_WIDTH),
        jax.ShapeDtypeStruct((nb, ATTN_WIDTH, t_all), BF16),
        bf_out(N_IDX_HEADS * LANES), bf_out(LANES),
        jax.ShapeDtypeStruct((nb, N_IDX_HEADS, t_all), F32),
        f32_out(ATTN_WIDTH), f32_out(D_MODEL), f32_out(D_MODEL),
    )
    out_specs = (
        row_blk(ATTN_WIDTH), row_blk(ATTN_WIDTH), row_blk(IDX_DIM), state_blk,
        row_blk(ATTN_WIDTH), row_blk(ATTN_WIDTH), col_blk(ATTN_WIDTH),
        row_blk(N_IDX_HEADS * LANES), row_blk(LANES), col_blk(N_IDX_HEADS),
        row_blk(ATTN_WIDTH), row_blk(D_MODEL), row_blk(D_MODEL),
    )
    return pl.pallas_call(
        functools.partial(_proj_kernel, groups=groups, rows=rows, n_row_tiles=n_row_tiles),
        grid=(nb, n_row_tiles),
        in_specs=[row_blk(d), state_blk] + [full(a) for a in weights],
        out_specs=out_specs,
        out_shape=out_shapes,
        scratch_shapes=[pltpu.VMEM((groups, CONV_K - 1, CONV_WIDTH), F32)],
        compiler_params=pltpu.CompilerParams(
            dimension_semantics=("arbitrary", "arbitrary"), vmem_limit_bytes=VMEM_LIMIT_BYTES),
        name="proj",
    )(x, conv_past, *weights)


def _attn_kernel(*refs, q_cols, n_valid_q, tk, past_len, n_keys, k_top):
    if past_len:
        (qb_ref, qib_ref, wit_ref, k_ref, vt_ref, kib_ref, kp_ref, vp_ref, kip_ref, o_ref,
         *scratch) = refs
    else:
        qb_ref, qib_ref, wit_ref, k_ref, vt_ref, kib_ref, o_ref, *scratch = refs
    sc_ref, grp_ref, thr_ref, cnt_ref, acc_ref, m_ref, l_ref = scratch
    i = pl.program_id(1)
    Q = q_cols
    n_key_pad = sc_ref.shape[0]
    n_groups = grp_ref.shape[0]
    n_past_tiles = past_len // tk

    class NewKeys:
        def __init__(self, j):
            self.off = pl.multiple_of((j - n_past_tiles) * tk, tk)

        def idx(self):
            return kib_ref[0, pl.ds(self.off, tk), :IDX_DIM]

        def k(self, hh):
            return k_ref[0, pl.ds(self.off, tk), hh * HEAD_DIM:(hh + 1) * HEAD_DIM]

        def vt(self, hh):
            return vt_ref[0, hh * HEAD_DIM:(hh + 1) * HEAD_DIM, pl.ds(self.off, tk)]

    class PastKeys:
        def __init__(self, j):
            self.off = pl.multiple_of(j * tk, tk)

        def idx(self):
            return kip_ref[0, 0, pl.ds(self.off, tk), :].astype(BF16)

        def k(self, hh):
            return kp_ref[0, 0, pl.ds(self.off, tk), hh, :].astype(BF16)

        def vt(self, hh):
            return vp_ref[0, 0, pl.ds(self.off, tk), hh, :].T.astype(BF16)

    def for_key_tiles(n_tiles, body):
        if n_past_tiles:
            lax.fori_loop(0, jnp.minimum(n_tiles, n_past_tiles),
                          lambda j, c: (body(j, PastKeys(j)), c)[1], 0)
        lax.fori_loop(n_past_tiles, n_tiles, lambda j, c: (body(j, NewKeys(j)), c)[1], 0)

    lane = lax.broadcasted_iota(I32, (1, Q), 1)
    qpos = past_len + i * Q + lane
    lim = jnp.minimum((qpos // CHUNK + 1) * CHUNK, n_keys)
    last_lim = jnp.minimum(((past_len + i * Q + n_valid_q - 1) // CHUNK + 1) * CHUNK, n_keys)
    n_tiles = (last_lim + tk - 1) // tk
    row0 = lax.broadcasted_iota(I32, (tk, Q), 0)

    n_part = 4 * SUBLANES

    def key_of(x):
        bits = pltpu.bitcast(x, I32)
        return jnp.where(bits < 0, INT_MIN - bits, bits)

    def float_of(key):
        return pltpu.bitcast(jnp.where(key < 0, INT_MIN - key, key), F32)

    def floor_avg(a, b):
        return (a >> 1) + (b >> 1) + (a & b & 1)

    def count(pred):
        def body(j, c):
            off = pl.multiple_of(j * tk, tk)
            hit = jnp.where(pred(sc_ref[pl.ds(off, tk), :], row0 + off), 1, 0)
            return c + jnp.sum(hit.reshape(tk // n_part, n_part, Q), axis=0)
        c = lax.fori_loop(0, n_tiles, body, jnp.zeros((n_part, Q), I32))
        return jnp.sum(c, axis=0, keepdims=True)

    def scores_and_candidate(hi_ref, lo_ref):
        wsc = wit_ref[0] * (IDX_DIM ** -0.5 * N_IDX_HEADS ** -0.5)
        grp_ref[...] = jnp.full(grp_ref.shape, -jnp.inf, F32)

        def score_tile(j, src):
            off = pl.multiple_of(j * tk, tk)
            kt = src.idx()
            s = jnp.zeros((tk, Q), F32)
            for hh in range(N_IDX_HEADS):
                qh = qib_ref[0, :, hh * LANES:hh * LANES + IDX_DIM]
                d = lax.dot_general(kt, qh, NT_DIMS, preferred_element_type=F32)
                s = s + jnp.maximum(d, 0.0) * wsc[hh:hh + 1, :]
            row = row0 + off
            s = jnp.where(row < lim, s, NEG_INF)
            if n_key_pad != n_keys:
                s = jnp.where(row < n_keys, s, -jnp.inf)
            sc_ref[pl.ds(off, tk), :] = s
            key = key_of(s)
            hi_ref[pl.ds(off, tk), :] = (key >> 16).astype(I16)
            lo_ref[pl.ds(off, tk), :] = ((key & 0xFFFF) - 2 ** 15).astype(I16)
            g = grp_ref[...]
            for c in range(tk // n_groups):
                g = jnp.maximum(g, s[c * n_groups:(c + 1) * n_groups])
            grp_ref[...] = g

        for_key_tiles(n_tiles, score_tile)

        def count16(plane_ref, cand):
            cand16 = cand.astype(I16)

            def body(j, c):
                off = pl.multiple_of(j * tk, tk)
                hit = jnp.where(plane_ref[pl.ds(off, tk), :] >= cand16, jnp.int16(1), jnp.int16(0))
                for g in range(tk // n_part):
                    c = c + hit[g * n_part:(g + 1) * n_part]
                return c

            c = lax.fori_loop(0, n_tiles, body, jnp.zeros((n_part, Q), I16))
            return jnp.sum(c.astype(I32), axis=0, keepdims=True)

        def bisect16(plane_ref, lo, hi, want, n_steps):
            def step(_, st):
                lo, hi, n_hi = st
                mid = (lo + hi) >> 1
                n = count16(plane_ref, mid)
                ge = n >= want
                return jnp.where(ge, mid, lo), jnp.where(ge, hi, mid), jnp.where(ge, n_hi, n)
            lo, _, n_hi = lax.fori_loop(0, n_steps, step, (lo, hi, jnp.zeros((1, Q), I32)))
            return lo, n_hi

        g = grp_ref[...]
        lo0 = key_of(jnp.min(g, axis=0, keepdims=True)) >> 16
        hi0 = (key_of(jnp.max(g, axis=0, keepdims=True)) >> 16) + 1
        n_steps = jnp.max(32 - lax.clz(hi0 - lo0 - 1))
        h_thr, n_above = bisect16(hi_ref, lo0, hi0, k_top, n_steps)
        h_thr16 = h_thr.astype(I16)

        def park(j, carry):
            off = pl.multiple_of(j * tk, tk)
            lo_ref[pl.ds(off, tk), :] = jnp.where(hi_ref[pl.ds(off, tk), :] == h_thr16,
                                                  lo_ref[pl.ds(off, tk), :], jnp.int16(-2 ** 15))
            return carry

        lax.fori_loop(0, n_tiles, park, 0)
        l_thr, _ = bisect16(lo_ref, jnp.full((1, Q), -2 ** 15, I32),
                            jnp.full((1, Q), 2 ** 15, I32), k_top - n_above, 16)
        cand = float_of(h_thr * 2 ** 16 + (l_thr + 2 ** 15))
        thr_ref[...] = jnp.broadcast_to(cand, thr_ref.shape)

    plane16 = pltpu.VMEM((n_key_pad, Q), I16)
    pl.run_scoped(scores_and_candidate, plane16, plane16)

    def bisect(lo, hi, n_steps):
        def step(_, st):
            lo, hi = st
            mid = floor_avg(lo, hi)
            t = float_of(mid)
            ge = count(lambda s, row: s >= t) >= k_top
            return jnp.where(ge, mid, lo), jnp.where(ge, hi, mid)
        return lax.fori_loop(0, n_steps, step, (lo, hi))[0]

    def count_ge_gt(t):
        def body(j, c):
            off = pl.multiple_of(j * tk, tk)
            s = sc_ref[pl.ds(off, tk), :]
            fold = lambda hit: jnp.sum(hit.reshape(tk // n_part, n_part, Q), axis=0)
            return c[0] + fold(jnp.where(s >= t, 1, 0)), c[1] + fold(jnp.where(s > t, 1, 0))
        z = jnp.zeros((n_part, Q), I32)
        ge, gt = lax.fori_loop(0, n_tiles, body, (z, z))
        return jnp.sum(ge, axis=0, keepdims=True), jnp.sum(gt, axis=0, keepdims=True)

    def publish(t, n_ge, n_gt):
        thr_ref[...] = jnp.broadcast_to(t, thr_ref.shape)
        cnt_ref[0] = jnp.broadcast_to(n_ge, cnt_ref.shape[1:])
        cnt_ref[1] = jnp.broadcast_to(n_gt, cnt_ref.shape[1:])

    valid_q = lane < n_valid_q
    cand = thr_ref[0:1, :]
    n_ge, n_gt = count_ge_gt(cand)
    publish(cand, n_ge, n_gt)
    refuted = jnp.max(jnp.where(((n_ge < k_top) | (n_gt >= k_top)) & valid_q, 1, 0)) > 0

    @pl.when(refuted)
    def _():
        def min_where(pred):
            def body(j, m):
                off = pl.multiple_of(j * tk, tk)
                s = sc_ref[pl.ds(off, tk), :]
                hit = jnp.where(pred(s), s, jnp.inf)
                return jnp.minimum(m, jnp.min(hit.reshape(tk // n_part, n_part, Q), axis=0))
            m = lax.fori_loop(0, n_tiles, body, jnp.full((n_part, Q), jnp.inf, F32))
            return jnp.min(m, axis=0, keepdims=True)

        full_lo = jnp.full((1, Q), -KEY_INF, I32)
        t_grid = float_of(bisect(full_lo, jnp.full((1, Q), KEY_INF + 1, I32), 32))
        t0 = min_where(lambda s: s >= t_grid)

        def crowded(st):
            return jnp.max(jnp.where(st[2] >= k_top, 1, 0)) > 0

        def raise_thr(st):
            t, _, n_gt = st
            t = jnp.where(n_gt >= k_top, min_where(lambda s: s > t), t)
            return (t,) + count_ge_gt(t)

        publish(*lax.while_loop(crowded, raise_thr, (t0,) + count_ge_gt(t0)))

    thr = thr_ref[0:1, :]
    n_ge, n_gt = cnt_ref[0, 0:1, :], cnt_ref[1, 0:1, :]

    untidy = jnp.max(jnp.where(((n_ge > k_top) | (thr <= NEG_INF)) & valid_q, 1, 0)) > 0

    @pl.when(untidy)
    def _():
        want = (k_top - n_gt).astype(F32)
        tri = jnp.where(lax.broadcasted_iota(I32, (tk, tk), 0) >= lax.broadcasted_iota(I32, (tk, tk), 1),
                        1.0, 0.0).astype(BF16)

        def drop_tile(j, seen):
            off = pl.multiple_of(j * tk, tk)
            s = sc_ref[pl.ds(off, tk), :]
            tied = s == thr
            rank = jnp.dot(tri, jnp.where(tied, 1.0, 0.0).astype(BF16),
                           preferred_element_type=F32) + seen
            drop = (tied & (rank > want)) | (row0 + off >= lim)
            sc_ref[pl.ds(off, tk), :] = jnp.where(drop, -jnp.inf, s)
            return rank[tk - 1:tk, :]

        lax.fori_loop(0, n_tiles, drop_tile, jnp.zeros((1, Q), F32))

    m_ref[...] = jnp.full(m_ref.shape, NEG_INF, F32)
    l_ref[...] = jnp.zeros(l_ref.shape, F32)
    acc_ref[...] = jnp.zeros(acc_ref.shape, F32)
    ones_rows = jnp.ones((2 * SUBLANES, tk), BF16)

    def attend(bias_ref, s_ref):
        def attn_tile(j, src):
            off = pl.multiple_of(j * tk, tk)
            bias_ref[...] = jnp.where(sc_ref[pl.ds(off, tk), :] >= thr, 0.0, NEG_INF)

            def logits(hh):
                lo, hi = hh * HEAD_DIM, (hh + 1) * HEAD_DIM
                s = lax.dot_general(src.k(hh), qb_ref[0, :, lo:hi], NT_DIMS,
                                    preferred_element_type=F32) + bias_ref[...]
                s_ref[hh] = s
                return jnp.maximum(m_ref[hh], jnp.max(s, axis=0, keepdims=True))

            def accumulate(hh, m_new):
                lo, hi = hh * HEAD_DIM, (hh + 1) * HEAD_DIM
                alpha = jnp.exp2(m_ref[hh] - m_new)
                p = jnp.exp2(s_ref[hh] - m_new[0:1])
                vt1 = jnp.concatenate([src.vt(hh), ones_rows], axis=0)
                pv = jnp.dot(vt1, p.astype(BF16), preferred_element_type=F32)
                l_ref[hh] = alpha * l_ref[hh] + pv[HEAD_DIM:HEAD_DIM + SUBLANES]
                acc_ref[lo:hi, :] = alpha[0:1] * acc_ref[lo:hi, :] + pv[:HEAD_DIM]
                m_ref[hh] = m_new

            m_news = [logits(hh) for hh in range(HEAD_LOOKAHEAD)]
            for hh in range(N_HEADS):
                if hh + HEAD_LOOKAHEAD < N_HEADS:
                    m_news.append(logits(hh + HEAD_LOOKAHEAD))
                accumulate(hh, m_news[hh])

        for_key_tiles(n_tiles, attn_tile)

    pl.run_scoped(attend, pltpu.VMEM((tk, Q), F32), pltpu.VMEM((N_HEADS, tk, Q), F32))

    for hh in range(N_HEADS):
        lo, hi = hh * HEAD_DIM, (hh + 1) * HEAD_DIM
        acc_ref[lo:hi, :] = acc_ref[lo:hi, :] / l_ref[hh, 0:1, :]
    o_ref[0] = acc_ref[...].T


def _attn_call(qb, qib, wit, k_new, vt_new, kib_new, past, layer, *, q_cols, n_valid_q, tk,
               n_keys, k_top):
    nb, t_all, width = qb.shape
    past_len = past[0].shape[2] if past else 0
    n_new_pad = k_new.shape[1]
    n_key_pad = past_len + n_new_pad
    n_groups = 2 * LANES
    assert t_all % q_cols == 0 and n_new_pad % tk == 0 and past_len % tk == 0
    assert tk % n_groups == 0 and n_groups >= k_top and n_key_pad >= n_keys
    nq = t_all // q_cols
    q_blk = lambda w_: pl.BlockSpec((1, q_cols, w_), lambda b, i: (b, i, 0))
    keys_blk = _resident if nq > 1 else pl.BlockSpec
    stream_blk = lambda a: keys_blk((1,) + a.shape[1:], lambda b, i: (b, 0, 0))
    return pl.pallas_call(
        functools.partial(_attn_kernel, q_cols=q_cols, n_valid_q=n_valid_q, tk=tk,
                          past_len=past_len, n_keys=n_keys, k_top=k_top),
        grid=(nb, nq),
        in_specs=[q_blk(width), q_blk(N_IDX_HEADS * LANES),
                  pl.BlockSpec((1, N_IDX_HEADS, q_cols), lambda b, i: (b, 0, i)),
                  stream_blk(k_new), stream_blk(vt_new), stream_blk(kib_new)]
                 + [pl.BlockSpec((1, 1) + a.shape[2:],
                                 lambda b, i, nd=a.ndim: (layer, b) + (0,) * (nd - 2)) for a in past],
        out_specs=q_blk(width),
        out_shape=jax.ShapeDtypeStruct((nb, t_all, width), F32),
        scratch_shapes=[pltpu.VMEM((n_key_pad, q_cols), F32),
                        pltpu.VMEM((n_groups, q_cols), F32),
                        pltpu.VMEM((SUBLANES, q_cols), F32),
                        pltpu.VMEM((2, SUBLANES, q_cols), I32),
                        pltpu.VMEM((width, q_cols), F32),
                        pltpu.VMEM((N_HEADS, SUBLANES, q_cols), F32),
                        pltpu.VMEM((N_HEADS, SUBLANES, q_cols), F32)],
        compiler_params=pltpu.CompilerParams(
            dimension_semantics=("arbitrary", "arbitrary"), vmem_limit_bytes=VMEM_LIMIT_BYTES),
        name="attn",
    )(qb, qib, wit, k_new, vt_new, kib_new, *past)


def _post_kernel(attn_ref, sza_ref, ga_ref, gyb_ref, x_ref, waout_ref, wout_ref, npost_ref, y_ref):
    ya_in = (sza_ref[...] * attn_ref[...]).astype(BF16)
    y_a = jnp.dot(ya_in, waout_ref[...], preferred_element_type=F32)
    merged = ga_ref[...] * y_a + gyb_ref[...]
    out = jnp.dot(merged.astype(BF16), wout_ref[...], preferred_element_type=F32)
    ms = jnp.mean(out * out, axis=-1, keepdims=True)
    y_ref[...] = x_ref[...] + (out * lax.rsqrt(ms + EPS)) * npost_ref[...]


def _post_call(attn, sza, ga, gyb, x, w, *, rows):
    n, d = x.shape
    assert n % rows == 0
    row_blk = pl.BlockSpec((rows, d), lambda t: (t, 0))
    full = lambda a: _resident(a.shape, lambda t: (0,) * a.ndim)
    weights = (w["w_a_out"], w["w_out"], w["norm_post"])
    return pl.pallas_call(
        _post_kernel,
        grid=(n // rows,),
        in_specs=[row_blk] * 5 + [full(a) for a in weights],
        out_specs=row_blk,
        out_shape=jax.ShapeDtypeStruct((n, d), F32),
        compiler_params=pltpu.CompilerParams(
            dimension_semantics=("arbitrary",), vmem_limit_bytes=VMEM_LIMIT_BYTES),
        name="post",
    )(attn, sza, ga, gyb, x, *weights)


def _prep_weights(norm_pre, w_in, conv_w, w_a_out, w_b_out, b_merge, w_out, norm_post):
    W, C, D = ATTN_WIDTH, CONV_WIDTH, D_MODEL
    o_qi = 4 * W
    o_ki = o_qi + N_IDX_HEADS * IDX_DIM
    o_wi = o_ki + IDX_DIM
    o_conv = o_wi + N_IDX_HEADS
    o_g = o_conv + 4 * C
    bf = lambda a: a.astype(BF16)
    w_qi = w_in[:, o_qi:o_ki].reshape(D, N_IDX_HEADS, IDX_DIM)
    w_qi = jnp.pad(w_qi, ((0, 0), (0, 0), (0, LANES - IDX_DIM))).reshape(D, N_IDX_HEADS * LANES)
    return {
        "norm_pre": norm_pre.reshape(1, D),
        "w_qkvz": bf(w_in[:, :o_qi]),
        "w_qi": bf(w_qi),
        "w_ki": bf(jnp.pad(w_in[:, o_ki:o_wi], ((0, 0), (0, LANES - IDX_DIM)))),
        "w_wit": bf(jnp.pad(w_in[:, o_wi:o_conv].T, ((0, LANES - N_IDX_HEADS), (0, 0)))),
        "w_conv": bf(w_in[:, o_conv:o_g]),
        "w_g": bf(w_in[:, o_g:]),
        "conv_w": conv_w,
        "w_b_out": bf(w_b_out),
        "b_merge": b_merge,
        "w_a_out": bf(w_a_out),
        "w_out": bf(w_out),
        "norm_post": norm_post.reshape(1, D),
    }


def _round_up(a, b):
    return (a + b - 1) // b * b


def _hybrid_layer(x, past, layer, conv_past, w, *, flatten_streams):
    B, T, D = x.shape
    P = past[0].shape[2] if past else 0
    n_keys = P + T
    k_top = min(TOPK_MAX, n_keys // 4)

    if flatten_streams:
        outs = _proj_call(x.reshape(1, B * T, D), conv_past, w, groups=B, rows=T, n_row_tiles=1)
    else:
        rows = min(T, 256)
        outs = _proj_call(x, conv_past, w, groups=1, rows=rows, n_row_tiles=T // rows)
    k, v, ki, cstate, qb, kb, vtb, qib, kib, wit, sza, ga, gyb = outs
    per_stream = lambda a: a.reshape(B, T, a.shape[-1])
    k, v, ki, qb, kb, qib, kib = map(per_stream, (k, v, ki, qb, kb, qib, kib))
    if flatten_streams:
        per_stream_t = lambda a: a.reshape(a.shape[1], B, T).transpose(1, 0, 2)
        vtb, wit = per_stream_t(vtb), per_stream_t(wit)

    q_cols = min(_round_up(T, LANES), 2 * LANES)
    t_pad = _round_up(T, q_cols)
    tk = 512 if T % 512 == 0 and P % 512 == 0 else 2 * LANES
    n_new_pad = _round_up(T, tk)
    if n_new_pad != T:
        pad_rows = ((0, 0), (0, n_new_pad - T), (0, 0))
        kb, kib = jnp.pad(kb, pad_rows), jnp.pad(kib, pad_rows)
        vtb = jnp.pad(vtb, ((0, 0), (0, 0), (0, n_new_pad - T)))
    if t_pad != T:
        pad_q = ((0, 0), (0, t_pad - T), (0, 0))
        qb, qib = jnp.pad(qb, pad_q), jnp.pad(qib, pad_q)
        wit = jnp.pad(wit, ((0, 0), (0, 0), (0, t_pad - T)))
    attn = _attn_call(qb, qib, wit, kb, vtb, kib, past, layer, q_cols=q_cols,
                      n_valid_q=min(T, q_cols), tk=tk, n_keys=n_keys, k_top=k_top)
    attn = attn[:, :T]

    flat = lambda a: a.reshape(B * T, a.shape[-1])
    y = _post_call(flat(attn), flat(sza), flat(ga), flat(gyb), flat(x), w, rows=min(B * T, 512))
    return (y.reshape(B, T, D), k.reshape(B, T, N_HEADS, HEAD_DIM),
            v.reshape(B, T, N_HEADS, HEAD_DIM), ki, cstate)


def kernel(x_prompt, x_sample, cache_k, cache_v, cache_kidx, state_conv, norm_pre, w_in, conv_w,
           w_a_out, w_b_out, b_merge, w_out, norm_post):
    B = x_prompt.shape[0]
    zero_conv = jnp.zeros((B, CONV_K - 1, CONV_WIDTH), x_prompt.dtype)
    hp, hs = x_prompt, x_sample
    caches = (cache_k, cache_v, cache_kidx)
    per_layer = []
    for layer in range(w_in.shape[0]):
        w = _prep_weights(norm_pre[layer], w_in[layer], conv_w[layer], w_a_out[layer],
                          w_b_out[layer], b_merge[layer], w_out[layer], norm_post[layer])
        hp, k1, v1, ki1, c1 = _hybrid_layer(hp, (), 0, zero_conv, w, flatten_streams=False)
        hs, k2, v2, ki2, c2 = _hybrid_layer(hs, caches, layer, state_conv[layer], w,
                                            flatten_streams=True)
        per_layer.append((k1, v1, ki1, c1, k2, v2, ki2, c2))
    return (hp, hs) + tuple(jnp.stack(leaf) for leaf in zip(*per_layer))
```

```python
import functools

import numpy as np
import jax
import jax.numpy as jnp
from jax import lax
from jax.experimental import pallas as pl
from jax.experimental.pallas import tpu as pltpu

D_MODEL = 1024
CHUNK = 64
N_HEADS = 8
HEAD_DIM = 128
ATTN_WIDTH = N_HEADS * HEAD_DIM
N_IDX_HEADS = 8
IDX_DIM = 64
TOPK_MAX = 256
CONV_WIDTH = 1024
CONV_K = 3
EPS = 1e-6
NEG_INF = -1e30

LANES = 128
SUBLANES = 8
VMEM_LIMIT_BYTES = 60 * 1024 * 1024

INT_BITS = 32
INT_MIN = -(2 ** (INT_BITS - 1))
HALF_BITS = INT_BITS // 2
HALF_BIAS = 2 ** (HALF_BITS - 1)
KEY_INF = 0x7F800000

F32 = jnp.float32
BF16 = jnp.bfloat16
I32 = jnp.int32
I16 = jnp.int16
NT_DIMS = (((1,), (1,)), ((), ()))
QK_SCALE_LOG2 = HEAD_DIM ** -0.5 * float(np.log2(np.e))
HEAD_LOOKAHEAD = 2


def _resident(shape, index_map):
    return pl.BlockSpec(shape, index_map, pipeline_mode=pl.Buffered(1))


def _proj_kernel(x_ref, past_ref, npre_ref, wqkvz_ref, wqi_ref, wki_ref, wwit_ref, wconv_ref,
                 wg_ref, convw_ref, wbout_ref, bmerge_ref,
                 k_ref, v_ref, ki_ref, cstate_ref, qb_ref, kb_ref, vtb_ref, qib_ref, kib_ref,
                 wit_ref, sza_ref, ga_ref, gyb_ref, carry_ref, *, groups, rows, n_row_tiles):
    t = pl.program_id(1)
    x = x_ref[0]
    ms = jnp.mean(x * x, axis=-1, keepdims=True)
    h = ((x * lax.rsqrt(ms + EPS)) * npre_ref[...]).astype(BF16)

    def proj(w_ref, lo, hi):
        return jnp.dot(h, w_ref[:, lo:hi], preferred_element_type=F32)

    W = ATTN_WIDTH
    q = proj(wqkvz_ref, 0, W)
    qb_ref[0] = (q * QK_SCALE_LOG2).astype(BF16)
    k = proj(wqkvz_ref, W, 2 * W)
    k_ref[0] = k
    kb_ref[0] = k.astype(BF16)
    v = proj(wqkvz_ref, 2 * W, 3 * W)
    v_ref[0] = v
    vtb_ref[0] = v.T.astype(BF16)
    z_a = proj(wqkvz_ref, 3 * W, 4 * W)
    sza_ref[0] = z_a * jax.nn.sigmoid(z_a)
    qib_ref[0] = proj(wqi_ref, 0, N_IDX_HEADS * LANES).astype(BF16)
    ki = proj(wki_ref, 0, LANES)
    ki_ref[0] = ki[:, :IDX_DIM]
    kib_ref[0] = ki.astype(BF16)
    wit = lax.dot_general(wwit_ref[...], h, NT_DIMS, preferred_element_type=F32)
    wit_ref[0] = wit[:N_IDX_HEADS]

    C = CONV_WIDTH
    u = proj(wconv_ref, 0, C)
    gate_c = proj(wconv_ref, 2 * C, 3 * C)
    cin = gate_c * u
    gate_b = proj(wconv_ref, C, 2 * C)
    z_b = proj(wconv_ref, 3 * C, 4 * C)
    pre = (z_b * jax.nn.sigmoid(z_b)) * gate_b

    @pl.when(t == 0)
    def _():
        carry_ref[...] = past_ref[...]

    w0 = convw_ref[0:1, :]
    w1 = convw_ref[1:2, :]
    w2 = convw_ref[2:3, :]
    row = lax.broadcasted_iota(I32, (rows, C), 0)
    convs = []
    for g in range(groups):
        cg = cin[g * rows:(g + 1) * rows]
        past0 = carry_ref[g, 0:1, :]
        past1 = carry_ref[g, 1:2, :]
        r1 = jnp.where(row == 0, past1, pltpu.roll(cg, 1, 0))
        r2 = jnp.where(row == 0, past0, jnp.where(row == 1, past1, pltpu.roll(cg, 2, 0)))
        convs.append(w0 * r2 + w1 * r1 + w2 * cg)
        carry_ref[g] = cg[rows - (CONV_K - 1):rows]
    conv = convs[0] if groups == 1 else jnp.concatenate(convs, axis=0)

    @pl.when(t == n_row_tiles - 1)
    def _():
        cstate_ref[...] = carry_ref[...]

    y_b = jnp.dot((pre * conv).astype(BF16), wbout_ref[...], preferred_element_type=F32)
    g_a = proj(wg_ref, 0, D_MODEL)
    ga_ref[0] = jax.nn.sigmoid(g_a + bmerge_ref[0:1, :])
    g_b = proj(wg_ref, D_MODEL, 2 * D_MODEL)
    gyb_ref[0] = jax.nn.sigmoid(g_b + bmerge_ref[1:2, :]) * y_b


def _proj_call(x, conv_past, w, *, groups, rows, n_row_tiles):
    nb, t_all, d = x.shape
    m = groups * rows
    assert t_all == n_row_tiles * m and (groups == 1 or n_row_tiles == 1)
    row_blk = lambda width: pl.BlockSpec((1, m, width), lambda b, t: (b, t, 0))
    col_blk = lambda height: pl.BlockSpec((1, height, m), lambda b, t: (b, 0, t))
    full = lambda a: _resident(a.shape, lambda b, t: (0,) * a.ndim)
    state_blk = pl.BlockSpec((groups, CONV_K - 1, CONV_WIDTH), lambda b, t: (b, 0, 0))
    f32_out = lambda width: jax.ShapeDtypeStruct((nb, t_all, width), F32)
    bf_out = lambda width: jax.ShapeDtypeStruct((nb, t_all, width), BF16)
    weights = (w["norm_pre"], w["w_qkvz"], w["w_qi"], w["w_ki"], w["w_wit"], w["w_conv"], w["w_g"],
               w["conv_w"], w["w_b_out"], w["b_merge"])
    out_shapes = (
        f32_out(ATTN_WIDTH), f32_out(ATTN_WIDTH), f32_out(IDX_DIM),
        jax.ShapeDtypeStruct(conv_past.shape, F32),
        bf_out(ATTN_WIDTH), bf_out(ATTN_WIDTH),
        jax.ShapeDtypeStruct((nb, ATTN_WIDTH, t_all), BF16),
        bf_out(N_IDX_HEADS * LANES), bf_out(LANES),
        jax.ShapeDtypeStruct((nb, N_IDX_HEADS, t_all), F32),
        f32_out(ATTN_WIDTH), f32_out(D_MODEL), f32_out(D_MODEL),
    )
    out_specs = (
        row_blk(ATTN_WIDTH), row_blk(ATTN_WIDTH), row_blk(IDX_DIM), state_blk,
        row_blk(ATTN_WIDTH), row_blk(ATTN_WIDTH), col_blk(ATTN_WIDTH),
        row_blk(N_IDX_HEADS * LANES), row_blk(LANES), col_blk(N_IDX_HEADS),
        row_blk(ATTN_WIDTH), row_blk(D_MODEL), row_blk(D_MODEL),
    )
    return pl.pallas_call(
        functools.partial(_proj_kernel, groups=groups, rows=rows, n_row_tiles=n_row_tiles),
        grid=(nb, n_row_tiles),
        in_specs=[row_blk(d), state_blk] + [full(a) for a in weights],
        out_specs=out_specs,
        out_shape=out_shapes,
        scratch_shapes=[pltpu.VMEM((groups, CONV_K - 1, CONV_WIDTH), F32)],
        compiler_params=pltpu.CompilerParams(
            dimension_semantics=("arbitrary", "arbitrary"), vmem_limit_bytes=VMEM_LIMIT_BYTES),
        name="proj",
    )(x, conv_past, *weights)


def _attn_kernel(*refs, q_cols, n_valid_q, tk, past_len, n_keys, k_top):
    if past_len:
        (qb_ref, qib_ref, wit_ref, k_ref, vt_ref, kib_ref, kp_ref, vp_ref, kip_ref, o_ref,
         *scratch) = refs
    else:
        qb_ref, qib_ref, wit_ref, k_ref, vt_ref, kib_ref, o_ref, *scratch = refs
    sc_ref, grp_ref, thr_ref, cnt_ref, acc_ref, m_ref, l_ref = scratch
    i = pl.program_id(1)
    Q = q_cols
    n_key_pad = sc_ref.shape[0]
    n_groups = grp_ref.shape[0]
    n_past_tiles = past_len // tk

    class NewKeys:
        def __init__(self, j):
            self.off = pl.multiple_of((j - n_past_tiles) * tk, tk)

        def idx(self):
            return kib_ref[0, pl.ds(self.off, tk), :IDX_DIM]

        def k(self, hh):
            return k_ref[0, pl.ds(self.off, tk), hh * HEAD_DIM:(hh + 1) * HEAD_DIM]

        def vt(self, hh):
            return vt_ref[0, hh * HEAD_DIM:(hh + 1) * HEAD_DIM, pl.ds(self.off, tk)]

    class PastKeys:
        def __init__(self, j):
            self.off = pl.multiple_of(j * tk, tk)

        def idx(self):
            return kip_ref[0, 0, pl.ds(self.off, tk), :].astype(BF16)

        def k(self, hh):
            return kp_ref[0, 0, pl.ds(self.off, tk), hh, :].astype(BF16)

        def vt(self, hh):
            return vp_ref[0, 0, pl.ds(self.off, tk), hh, :].T.astype(BF16)

    def for_key_tiles(n_tiles, body):
        if n_past_tiles:
            lax.fori_loop(0, jnp.minimum(n_tiles, n_past_tiles),
                          lambda j, c: (body(j, PastKeys(j)), c)[1], 0)
        lax.fori_loop(n_past_tiles, n_tiles, lambda j, c: (body(j, NewKeys(j)), c)[1], 0)

    lane = lax.broadcasted_iota(I32, (1, Q), 1)
    qpos = past_len + i * Q + lane
    lim = jnp.minimum((qpos // CHUNK + 1) * CHUNK, n_keys)
    last_lim = jnp.minimum(((past_len + i * Q + n_valid_q - 1) // CHUNK + 1) * CHUNK, n_keys)
    n_tiles = (last_lim + tk - 1) // tk
    row0 = lax.broadcasted_iota(I32, (tk, Q), 0)

    n_part = 4 * SUBLANES

    def key_of(x):
        bits = pltpu.bitcast(x, I32)
        return jnp.where(bits < 0, INT_MIN - bits, bits)

    def float_of(key):
        return pltpu.bitcast(jnp.where(key < 0, INT_MIN - key, key), F32)

    def floor_avg(a, b):
        return (a >> 1) + (b >> 1) + (a & b & 1)

    def count(pred):
        def body(j, c):
            off = pl.multiple_of(j * tk, tk)
            hit = jnp.where(pred(sc_ref[pl.ds(off, tk), :], row0 + off), 1, 0)
            return c + jnp.sum(hit.reshape(tk // n_part, n_part, Q), axis=0)
        c = lax.fori_loop(0, n_tiles, body, jnp.zeros((n_part, Q), I32))
        return jnp.sum(c, axis=0, keepdims=True)

    def scores_and_candidate(hi_ref, lo_ref):
        wsc = wit_ref[0] * (IDX_DIM ** -0.5 * N_IDX_HEADS ** -0.5)
        grp_ref[...] = jnp.full(grp_ref.shape, -jnp.inf, F32)

        def score_tile(j, src):
            off = pl.multiple_of(j * tk, tk)
            kt = src.idx()
            s = jnp.zeros((tk, Q), F32)
            for hh in range(N_IDX_HEADS):
                qh = qib_ref[0, :, hh * LANES:hh * LANES + IDX_DIM]
                d = lax.dot_general(kt, qh, NT_DIMS, preferred_element_type=F32)
                s = s + jnp.maximum(d, 0.0) * wsc[hh:hh + 1, :]
            row = row0 + off
            s = jnp.where(row < lim, s, NEG_INF)
            if n_key_pad != n_keys:
                s = jnp.where(row < n_keys, s, -jnp.inf)
            sc_ref[pl.ds(off, tk), :] = s
            key = key_of(s)
            hi_ref[pl.ds(off, tk), :] = (key >> HALF_BITS).astype(I16)
            lo_ref[pl.ds(off, tk), :] = ((key & (2 * HALF_BIAS - 1)) - HALF_BIAS).astype(I16)
            g = grp_ref[...]
            for c in range(tk // n_groups):
                g = jnp.maximum(g, s[c * n_groups:(c + 1) * n_groups])
            grp_ref[...] = g

        for_key_tiles(n_tiles, score_tile)

        def count16(plane_ref, cand):
            cand16 = cand.astype(I16)

            def body(j, c):
                off = pl.multiple_of(j * tk, tk)
                hit = jnp.where(plane_ref[pl.ds(off, tk), :] >= cand16, jnp.int16(1), jnp.int16(0))
                for g in range(tk // n_part):
                    c = c + hit[g * n_part:(g + 1) * n_part]
                return c

            c = lax.fori_loop(0, n_tiles, body, jnp.zeros((n_part, Q), I16))
            return jnp.sum(c.astype(I32), axis=0, keepdims=True)

        def bisect16(plane_ref, lo, hi, want, n_steps):
            def step(_, st):
                lo, hi, n_hi = st
                mid = (lo + hi) >> 1
                n = count16(plane_ref, mid)
                ge = n >= want
                return jnp.where(ge, mid, lo), jnp.where(ge, hi, mid), jnp.where(ge, n_hi, n)
            lo, _, n_hi = lax.fori_loop(0, n_steps, step, (lo, hi, jnp.zeros((1, Q), I32)))
            return lo, n_hi

        g = grp_ref[...]
        lo0 = key_of(jnp.min(g, axis=0, keepdims=True)) >> HALF_BITS
        hi0 = (key_of(jnp.max(g, axis=0, keepdims=True)) >> HALF_BITS) + 1
        n_steps = jnp.max(INT_BITS - lax.clz(hi0 - lo0 - 1))
        h_thr, n_above = bisect16(hi_ref, lo0, hi0, k_top, n_steps)
        h_thr16 = h_thr.astype(I16)

        def park(j, carry):
            off = pl.multiple_of(j * tk, tk)
            lo_ref[pl.ds(off, tk), :] = jnp.where(hi_ref[pl.ds(off, tk), :] == h_thr16,
                                                  lo_ref[pl.ds(off, tk), :], jnp.int16(-HALF_BIAS))
            return carry

        lax.fori_loop(0, n_tiles, park, 0)
        l_thr, _ = bisect16(lo_ref, jnp.full((1, Q), -HALF_BIAS, I32),
                            jnp.full((1, Q), HALF_BIAS, I32), k_top - n_above, HALF_BITS)
        cand = float_of(h_thr * (2 * HALF_BIAS) + (l_thr + HALF_BIAS))
        thr_ref[...] = jnp.broadcast_to(cand, thr_ref.shape)

    plane16 = pltpu.VMEM((n_key_pad, Q), I16)
    pl.run_scoped(scores_and_candidate, plane16, plane16)

    def bisect(lo, hi, n_steps):
        def step(_, st):
            lo, hi = st
            mid = floor_avg(lo, hi)
            t = float_of(mid)
            ge = count(lambda s, row: s >= t) >= k_top
            return jnp.where(ge, mid, lo), jnp.where(ge, hi, mid)
        return lax.fori_loop(0, n_steps, step, (lo, hi))[0]

    def count_ge_gt(t):
        def body(j, c):
            off = pl.multiple_of(j * tk, tk)
            s = sc_ref[pl.ds(off, tk), :]
            fold = lambda hit: jnp.sum(hit.reshape(tk // n_part, n_part, Q), axis=0)
            return c[0] + fold(jnp.where(s >= t, 1, 0)), c[1] + fold(jnp.where(s > t, 1, 0))
        z = jnp.zeros((n_part, Q), I32)
        ge, gt = lax.fori_loop(0, n_tiles, body, (z, z))
        return jnp.sum(ge, axis=0, keepdims=True), jnp.sum(gt, axis=0, keepdims=True)

    def publish(t, n_ge, n_gt):
        thr_ref[...] = jnp.broadcast_to(t, thr_ref.shape)
        cnt_ref[0] = jnp.broadcast_to(n_ge, cnt_ref.shape[1:])
        cnt_ref[1] = jnp.broadcast_to(n_gt, cnt_ref.shape[1:])

    valid_q = lane < n_valid_q
    cand = thr_ref[0:1, :]
    n_ge, n_gt = count_ge_gt(cand)
    publish(cand, n_ge, n_gt)
    refuted = jnp.max(jnp.where(((n_ge < k_top) | (n_gt >= k_top)) & valid_q, 1, 0)) > 0

    @pl.when(refuted)
    def _():
        def min_where(pred):
            def body(j, m):
                off = pl.multiple_of(j * tk, tk)
                s = sc_ref[pl.ds(off, tk), :]
                hit = jnp.where(pred(s), s, jnp.inf)
                return jnp.minimum(m, jnp.min(hit.reshape(tk // n_part, n_part, Q), axis=0))
            m = lax.fori_loop(0, n_tiles, body, jnp.full((n_part, Q), jnp.inf, F32))
            return jnp.min(m, axis=0, keepdims=True)

        full_lo = jnp.full((1, Q), -KEY_INF, I32)
        t_grid = float_of(bisect(full_lo, jnp.full((1, Q), KEY_INF + 1, I32), INT_BITS))
        t0 = min_where(lambda s: s >= t_grid)

        def crowded(st):
            return jnp.max(jnp.where(st[2] >= k_top, 1, 0)) > 0

        def raise_thr(st):
            t, _, n_gt = st
            t = jnp.where(n_gt >= k_top, min_where(lambda s: s > t), t)
            return (t,) + count_ge_gt(t)

        publish(*lax.while_loop(crowded, raise_thr, (t0,) + count_ge_gt(t0)))

    thr = thr_ref[0:1, :]
    n_ge, n_gt = cnt_ref[0, 0:1, :], cnt_ref[1, 0:1, :]

    untidy = jnp.max(jnp.where(((n_ge > k_top) | (thr <= NEG_INF)) & valid_q, 1, 0)) > 0

    @pl.when(untidy)
    def _():
        want = (k_top - n_gt).astype(F32)
        tri = jnp.where(lax.broadcasted_iota(I32, (LANES, LANES), 0)
                        >= lax.broadcasted_iota(I32, (LANES, LANES), 1), 1.0, 0.0).astype(BF16)

        def drop_tile(j, seen):
            off = pl.multiple_of(j * tk, tk)
            s = sc_ref[pl.ds(off, tk), :]
            tied = s == thr
            flag = jnp.where(tied, 1.0, 0.0).astype(BF16)
            local = [jnp.dot(tri, flag[c * LANES:(c + 1) * LANES], preferred_element_type=F32)
                     for c in range(tk // LANES)]
            ranks = []
            for r in local:
                ranks.append(r + seen)
                seen = ranks[-1][LANES - 1:LANES, :]
            rank = jnp.concatenate(ranks, axis=0)
            drop = (tied & (rank > want)) | (row0 + off >= lim)
            sc_ref[pl.ds(off, tk), :] = jnp.where(drop, -jnp.inf, s)
            return seen

        lax.fori_loop(0, n_tiles, drop_tile, jnp.zeros((1, Q), F32))

    m_ref[...] = jnp.full(m_ref.shape, NEG_INF, F32)
    l_ref[...] = jnp.zeros(l_ref.shape, F32)
    acc_ref[...] = jnp.zeros(acc_ref.shape, F32)
    ones_rows = jnp.ones((2 * SUBLANES, tk), BF16)

    def attend(bias_ref, s_ref):
        def set_bias(j):
            off = pl.multiple_of(j * tk, tk)
            bias_ref[...] = jnp.where(sc_ref[pl.ds(off, tk), :] >= thr, 0.0, NEG_INF)

        def logits(src, hh):
            lo, hi = hh * HEAD_DIM, (hh + 1) * HEAD_DIM
            s = lax.dot_general(src.k(hh), qb_ref[0, :, lo:hi], NT_DIMS,
                                preferred_element_type=F32) + bias_ref[...]
            s_ref[hh] = s
            return jnp.maximum(m_ref[hh], jnp.max(s, axis=0, keepdims=True))

        def accumulate(src, hh, m_new):
            lo, hi = hh * HEAD_DIM, (hh + 1) * HEAD_DIM
            alpha = jnp.exp2(m_ref[hh] - m_new)
            p = jnp.exp2(s_ref[hh] - m_new[0:1])
            vt1 = jnp.concatenate([src.vt(hh), ones_rows], axis=0)
            pv = jnp.dot(vt1, p.astype(BF16), preferred_element_type=F32)
            l_ref[hh] = alpha * l_ref[hh] + pv[HEAD_DIM:HEAD_DIM + SUBLANES]
            acc_ref[lo:hi, :] = alpha[0:1] * acc_ref[lo:hi, :] + pv[:HEAD_DIM]
            m_ref[hh] = m_new

        def attn_tile(j, src):
            set_bias(j)
            m_news = [logits(src, hh) for hh in range(HEAD_LOOKAHEAD)]
            for hh in range(N_HEADS):
                if hh + HEAD_LOOKAHEAD < N_HEADS:
                    m_news.append(logits(src, hh + HEAD_LOOKAHEAD))
                accumulate(src, hh, m_news[hh])

        def attn_tile_pipelined(j, m_early):
            src, j_next = NewKeys(j), jnp.minimum(j + 1, n_tiles - 1)
            m_news, m_next = list(m_early), []
            for hh in range(N_HEADS):
                if hh + HEAD_LOOKAHEAD < N_HEADS:
                    m_news.append(logits(src, hh + HEAD_LOOKAHEAD))
                else:
                    if hh + HEAD_LOOKAHEAD == N_HEADS:
                        set_bias(j_next)
                    m_next.append(logits(NewKeys(j_next), hh + HEAD_LOOKAHEAD - N_HEADS))
                accumulate(src, hh, m_news[hh])
            return tuple(m_next)

        if n_past_tiles:
            for_key_tiles(n_tiles, attn_tile)
        else:
            set_bias(0)
            first = tuple(logits(NewKeys(0), hh) for hh in range(HEAD_LOOKAHEAD))
            lax.fori_loop(0, n_tiles, attn_tile_pipelined, first)

    pl.run_scoped(attend, pltpu.VMEM((tk, Q), F32), pltpu.VMEM((N_HEADS, tk, Q), F32))

    for hh in range(N_HEADS):
        lo, hi = hh * HEAD_DIM, (hh + 1) * HEAD_DIM
        acc_ref[lo:hi, :] = acc_ref[lo:hi, :] / l_ref[hh, 0:1, :]
    o_ref[0] = acc_ref[...].T


def _attn_call(qb, qib, wit, k_new, vt_new, kib_new, past, layer, *, q_cols, n_valid_q, tk,
               n_keys, k_top):
    nb, t_all, width = qb.shape
    past_len = past[0].shape[2] if past else 0
    n_new_pad = k_new.shape[1]
    n_key_pad = past_len + n_new_pad
    n_groups = 2 * LANES
    assert t_all % q_cols == 0 and n_new_pad % tk == 0 and past_len % tk == 0
    assert tk % n_groups == 0 and n_groups >= k_top and n_key_pad >= n_keys
    nq = t_all // q_cols
    q_blk = lambda w_: pl.BlockSpec((1, q_cols, w_), lambda b, i: (b, i, 0))
    keys_blk = _resident if nq > 1 else pl.BlockSpec
    stream_blk = lambda a: keys_blk((1,) + a.shape[1:], lambda b, i: (b, 0, 0))
    return pl.pallas_call(
        functools.partial(_attn_kernel, q_cols=q_cols, n_valid_q=n_valid_q, tk=tk,
                          past_len=past_len, n_keys=n_keys, k_top=k_top),
        grid=(nb, nq),
        in_specs=[q_blk(width), q_blk(N_IDX_HEADS * LANES),
                  pl.BlockSpec((1, N_IDX_HEADS, q_cols), lambda b, i: (b, 0, i)),
                  stream_blk(k_new), stream_blk(vt_new), stream_blk(kib_new)]
                 + [pl.BlockSpec((1, 1) + a.shape[2:],
                                 lambda b, i, nd=a.ndim: (layer, b) + (0,) * (nd - 2)) for a in past],
        out_specs=q_blk(width),
        out_shape=jax.ShapeDtypeStruct((nb, t_all, width), F32),
        scratch_shapes=[pltpu.VMEM((n_key_pad, q_cols), F32),
                        pltpu.VMEM((n_groups, q_cols), F32),
                        pltpu.VMEM((SUBLANES, q_cols), F32),
                        pltpu.VMEM((2, SUBLANES, q_cols), I32),
                        pltpu.VMEM((width, q_cols), F32),
                        pltpu.VMEM((N_HEADS, SUBLANES, q_cols), F32),
                        pltpu.VMEM((N_HEADS, SUBLANES, q_cols), F32)],
        compiler_params=pltpu.CompilerParams(
            dimension_semantics=("arbitrary", "arbitrary"), vmem_limit_bytes=VMEM_LIMIT_BYTES),
        name="attn",
    )(qb, qib, wit, k_new, vt_new, kib_new, *past)


def _post_kernel(attn_ref, sza_ref, ga_ref, gyb_ref, x_ref, waout_ref, wout_ref, npost_ref, y_ref):
    ya_in = (sza_ref[...] * attn_ref[...]).astype(BF16)
    y_a = jnp.dot(ya_in, waout_ref[...], preferred_element_type=F32)
    merged = ga_ref[...] * y_a + gyb_ref[...]
    out = jnp.dot(merged.astype(BF16), wout_ref[...], preferred_element_type=F32)
    ms = jnp.mean(out * out, axis=-1, keepdims=True)
    y_ref[...] = x_ref[...] + (out * lax.rsqrt(ms + EPS)) * npost_ref[...]


def _post_call(attn, sza, ga, gyb, x, w, *, rows):
    n, d = x.shape
    assert n % rows == 0
    row_blk = pl.BlockSpec((rows, d), lambda t: (t, 0))
    full = lambda a: _resident(a.shape, lambda t: (0,) * a.ndim)
    weights = (w["w_a_out"], w["w_out"], w["norm_post"])
    return pl.pallas_call(
        _post_kernel,
        grid=(n // rows,),
        in_specs=[row_blk] * 5 + [full(a) for a in weights],
        out_specs=row_blk,
        out_shape=jax.ShapeDtypeStruct((n, d), F32),
        compiler_params=pltpu.CompilerParams(
            dimension_semantics=("arbitrary",), vmem_limit_bytes=VMEM_LIMIT_BYTES),
        name="post",
    )(attn, sza, ga, gyb, x, *weights)


def _prep_weights(norm_pre, w_in, conv_w, w_a_out, w_b_out, b_merge, w_out, norm_post):
    W, C, D = ATTN_WIDTH, CONV_WIDTH, D_MODEL
    o_qi = 4 * W
    o_ki = o_qi + N_IDX_HEADS * IDX_DIM
    o_wi = o_ki + IDX_DIM
    o_conv = o_wi + N_IDX_HEADS
    o_g = o_conv + 4 * C
    bf = lambda a: a.astype(BF16)
    w_qi = w_in[:, o_qi:o_ki].reshape(D, N_IDX_HEADS, IDX_DIM)
    w_qi = jnp.pad(w_qi, ((0, 0), (0, 0), (0, LANES - IDX_DIM))).reshape(D, N_IDX_HEADS * LANES)
    return {
        "norm_pre": norm_pre.reshape(1, D),
        "w_qkvz": bf(w_in[:, :o_qi]),
        "w_qi": bf(w_qi),
        "w_ki": bf(jnp.pad(w_in[:, o_ki:o_wi], ((0, 0), (0, LANES - IDX_DIM)))),
        "w_wit": bf(jnp.pad(w_in[:, o_wi:o_conv].T, ((0, LANES - N_IDX_HEADS), (0, 0)))),
        "w_conv": bf(w_in[:, o_conv:o_g]),
        "w_g": bf(w_in[:, o_g:]),
        "conv_w": conv_w,
        "w_b_out": bf(w_b_out),
        "b_merge": b_merge,
        "w_a_out": bf(w_a_out),
        "w_out": bf(w_out),
        "norm_post": norm_post.reshape(1, D),
    }


def _round_up(a, b):
    return (a + b - 1) // b * b


def _hybrid_layer(x, past, layer, conv_past, w, *, flatten_streams):
    B, T, D = x.shape
    P = past[0].shape[2] if past else 0
    n_keys = P + T
    k_top = min(TOPK_MAX, n_keys // 4)

    if flatten_streams:
        outs = _proj_call(x.reshape(1, B * T, D), conv_past, w, groups=B, rows=T, n_row_tiles=1)
    else:
        rows = min(T, 256)
        outs = _proj_call(x, conv_past, w, groups=1, rows=rows, n_row_tiles=T // rows)
    k, v, ki, cstate, qb, kb, vtb, qib, kib, wit, sza, ga, gyb = outs
    per_stream = lambda a: a.reshape(B, T, a.shape[-1])
    k, v, ki, qb, kb, qib, kib = map(per_stream, (k, v, ki, qb, kb, qib, kib))
    if flatten_streams:
        per_stream_t = lambda a: a.reshape(a.shape[1], B, T).transpose(1, 0, 2)
        vtb, wit = per_stream_t(vtb), per_stream_t(wit)

    q_cols = min(_round_up(T, LANES), 2 * LANES)
    t_pad = _round_up(T, q_cols)
    tk = 512 if T % 512 == 0 and P % 512 == 0 else 2 * LANES
    n_new_pad = _round_up(T, tk)
    if n_new_pad != T:
        pad_rows = ((0, 0), (0, n_new_pad - T), (0, 0))
        kb, kib = jnp.pad(kb, pad_rows), jnp.pad(kib, pad_rows)
        vtb = jnp.pad(vtb, ((0, 0), (0, 0), (0, n_new_pad - T)))
    if t_pad != T:
        pad_q = ((0, 0), (0, t_pad - T), (0, 0))
        qb, qib = jnp.pad(qb, pad_q), jnp.pad(qib, pad_q)
        wit = jnp.pad(wit, ((0, 0), (0, 0), (0, t_pad - T)))
    attn = _attn_call(qb, qib, wit, kb, vtb, kib, past, layer, q_cols=q_cols,
                      n_valid_q=min(T, q_cols), tk=tk, n_keys=n_keys, k_top=k_top)
    attn = attn[:, :T]

    flat = lambda a: a.reshape(B * T, a.shape[-1])
    y = _post_call(flat(attn), flat(sza), flat(ga), flat(gyb), flat(x), w, rows=min(B * T, 512))
    return (y.reshape(B, T, D), k.reshape(B, T, N_HEADS, HEAD_DIM),
            v.reshape(B, T, N_HEADS, HEAD_DIM), ki, cstate)


def kernel(x_prompt, x_sample, cache_k, cache_v, cache_kidx, state_conv, norm_pre, w_in, conv_w,
           w_a_out, w_b_out, b_merge, w_out, norm_post):
    B = x_prompt.shape[0]
    zero_conv = jnp.zeros((B, CONV_K - 1, CONV_WIDTH), x_prompt.dtype)
    hp, hs = x_prompt, x_sample
    caches = (cache_k, cache_v, cache_kidx)
    per_layer = []
    for layer in range(w_in.shape[0]):
        w = _prep_weights(norm_pre[layer], w_in[layer], conv_w[layer], w_a_out[layer],
                          w_b_out[layer], b_merge[layer], w_out[layer], norm_post[layer])
        hp, k1, v1, ki1, c1 = _hybrid_layer(hp, (), 0, zero_conv, w, flatten_streams=False)
        hs, k2, v2, ki2, c2 = _hybrid_layer(hs, caches, layer, state_conv[layer], w,
                                            flatten_streams=True)
        per_layer.append((k1, v1, ki1, c1, k2, v2, ki2, c2))
    return (hp, hs) + tuple(jnp.stack(leaf) for leaf in zip(*per_layer))
```

```python
import functools

import numpy as np
import jax
import jax.numpy as jnp
from jax import lax
from jax.experimental import pallas as pl
from jax.experimental.pallas import tpu as pltpu

D_MODEL = 1024
CHUNK = 64
N_HEADS = 8
HEAD_DIM = 128
ATTN_WIDTH = N_HEADS * HEAD_DIM
N_IDX_HEADS = 8
IDX_DIM = 64
TOPK_MAX = 256
CONV_WIDTH = 1024
CONV_K = 3
EPS = 1e-6
NEG_INF = -1e30

LANES = 128
SUBLANES = 8
VMEM_LIMIT_BYTES = 60 * 1024 * 1024

INT_BITS = 32
INT_MIN = -(2 ** (INT_BITS - 1))
HALF_BITS = INT_BITS // 2
HALF_BIAS = 2 ** (HALF_BITS - 1)
KEY_INF = 0x7F800000

F32 = jnp.float32
BF16 = jnp.bfloat16
I32 = jnp.int32
I16 = jnp.int16
NT_DIMS = (((1,), (1,)), ((), ()))
QK_SCALE_LOG2 = HEAD_DIM ** -0.5 * float(np.log2(np.e))
HEAD_LOOKAHEAD = 2


def _resident(shape, index_map):
    return pl.BlockSpec(shape, index_map, pipeline_mode=pl.Buffered(1))


def _proj_kernel(x_ref, past_ref, npre_ref, wqkvz_ref, wqi_ref, wki_ref, wwit_ref, wconv_ref,
                 wg_ref, convw_ref, wbout_ref, bmerge_ref,
                 k_ref, v_ref, ki_ref, cstate_ref, qb_ref, kb_ref, vtb_ref, qib_ref, kib_ref,
                 wit_ref, sza_ref, ga_ref, gyb_ref, carry_ref, *, groups, rows, n_row_tiles):
    t = pl.program_id(1)
    x = x_ref[0]
    ms = jnp.mean(x * x, axis=-1, keepdims=True)
    h = ((x * lax.rsqrt(ms + EPS)) * npre_ref[...]).astype(BF16)

    def proj(w_ref, lo, hi):
        return jnp.dot(h, w_ref[:, lo:hi], preferred_element_type=F32)

    W = ATTN_WIDTH
    q = proj(wqkvz_ref, 0, W)
    qb_ref[0] = (q * QK_SCALE_LOG2).astype(BF16)
    k = proj(wqkvz_ref, W, 2 * W)
    k_ref[0] = k
    kb_ref[0] = k.astype(BF16)
    v = proj(wqkvz_ref, 2 * W, 3 * W)
    v_ref[0] = v
    vtb_ref[0] = v.T.astype(BF16)
    z_a = proj(wqkvz_ref, 3 * W, 4 * W)
    sza_ref[0] = z_a * jax.nn.sigmoid(z_a)
    qib_ref[0] = proj(wqi_ref, 0, N_IDX_HEADS * LANES).astype(BF16)
    ki = proj(wki_ref, 0, LANES)
    ki_ref[0] = ki[:, :IDX_DIM]
    kib_ref[0] = ki.astype(BF16)
    wit = lax.dot_general(wwit_ref[...], h, NT_DIMS, preferred_element_type=F32)
    wit_ref[0] = wit[:N_IDX_HEADS]

    C = CONV_WIDTH
    u = proj(wconv_ref, 0, C)
    gate_c = proj(wconv_ref, 2 * C, 3 * C)
    cin = gate_c * u
    gate_b = proj(wconv_ref, C, 2 * C)
    z_b = proj(wconv_ref, 3 * C, 4 * C)
    pre = (z_b * jax.nn.sigmoid(z_b)) * gate_b

    @pl.when(t == 0)
    def _():
        carry_ref[...] = past_ref[...]

    w0 = convw_ref[0:1, :]
    w1 = convw_ref[1:2, :]
    w2 = convw_ref[2:3, :]
    row = lax.broadcasted_iota(I32, (rows, C), 0)
    convs = []
    for g in range(groups):
        cg = cin[g * rows:(g + 1) * rows]
        past0 = carry_ref[g, 0:1, :]
        past1 = carry_ref[g, 1:2, :]
        r1 = jnp.where(row == 0, past1, pltpu.roll(cg, 1, 0))
        r2 = jnp.where(row == 0, past0, jnp.where(row == 1, past1, pltpu.roll(cg, 2, 0)))
        convs.append(w0 * r2 + w1 * r1 + w2 * cg)
        carry_ref[g] = cg[rows - (CONV_K - 1):rows]
    conv = convs[0] if groups == 1 else jnp.concatenate(convs, axis=0)

    @pl.when(t == n_row_tiles - 1)
    def _():
        cstate_ref[...] = carry_ref[...]

    y_b = jnp.dot((pre * conv).astype(BF16), wbout_ref[...], preferred_element_type=F32)
    g_a = proj(wg_ref, 0, D_MODEL)
    ga_ref[0] = jax.nn.sigmoid(g_a + bmerge_ref[0:1, :])
    g_b = proj(wg_ref, D_MODEL, 2 * D_MODEL)
    gyb_ref[0] = jax.nn.sigmoid(g_b + bmerge_ref[1:2, :]) * y_b


def _proj_call(x, conv_past, w, *, groups, rows, n_row_tiles):
    nb, t_all, d = x.shape
    m = groups * rows
    assert t_all == n_row_tiles * m and (groups == 1 or n_row_tiles == 1)
    row_blk = lambda width: pl.BlockSpec((1, m, width), lambda b, t: (b, t, 0))
    col_blk = lambda height: pl.BlockSpec((1, height, m), lambda b, t: (b, 0, t))
    full = lambda a: _resident(a.shape, lambda b, t: (0,) * a.ndim)
    state_blk = pl.BlockSpec((groups, CONV_K - 1, CONV_WIDTH), lambda b, t: (b, 0, 0))
    f32_out = lambda width: jax.ShapeDtypeStruct((nb, t_all, width), F32)
    bf_out = lambda width: jax.ShapeDtypeStruct((nb, t_all, width), BF16)
    weights = (w["norm_pre"], w["w_qkvz"], w["w_qi"], w["w_ki"], w["w_wit"], w["w_conv"], w["w_g"],
               w["conv_w"], w["w_b_out"], w["b_merge"])
    out_shapes = (
        f32_out(ATTN_WIDTH), f32_out(ATTN_WIDTH), f32_out(IDX_DIM),
        jax.ShapeDtypeStruct(conv_past.shape, F32),
        bf_out(ATTN_WIDTH), bf_out(ATTN_WIDTH),
        jax.ShapeDtypeStruct((nb, ATTN_WIDTH, t_all), BF16),
        bf_out(N_IDX_HEADS * LANES), bf_out(LANES),
        jax.ShapeDtypeStruct((nb, N_IDX_HEADS, t_all), F32),
        f32_out(ATTN_WIDTH), f32_out(D_MODEL), f32_out(D_MODEL),
    )
    out_specs = (
        row_blk(ATTN_WIDTH), row_blk(ATTN_WIDTH), row_blk(IDX_DIM), state_blk,
        row_blk(ATTN_WIDTH), row_blk(ATTN_WIDTH), col_blk(ATTN_WIDTH),
        row_blk(N_IDX_HEADS * LANES), row_blk(LANES), col_blk(N_IDX_HEADS),
        row_blk(ATTN_WIDTH), row_blk(D_MODEL), row_blk(D_MODEL),
    )
    return pl.pallas_call(
        functools.partial(_proj_kernel, groups=groups, rows=rows, n_row_tiles=n_row_tiles),
        grid=(nb, n_row_tiles),
        in_specs=[row_blk(d), state_blk] + [full(a) for a in weights],
        out_specs=out_specs,
        out_shape=out_shapes,
        scratch_shapes=[pltpu.VMEM((groups, CONV_K - 1, CONV_WIDTH), F32)],
        compiler_params=pltpu.CompilerParams(
            dimension_semantics=("arbitrary", "arbitrary"), vmem_limit_bytes=VMEM_LIMIT_BYTES),
        name="proj",
    )(x, conv_past, *weights)


def _attn_kernel(*refs, q_cols, n_valid_q, tk, past_len, n_keys, k_top, layer):
    if past_len:
        (qb_ref, qib_ref, wit_ref, k_ref, vt_ref, kib_ref, kp_hbm, vp_hbm, kip_ref, o_ref,
         sc_ref, grp_ref, thr_ref, cnt_ref, acc_ref, m_ref, l_ref, kp_ref, vp_ref, past_sem) = refs
        stream = pl.program_id(0)
        past_copies = [
            pltpu.make_async_copy(hbm.at[layer, stream, :, hh, :], buf.at[hh], past_sem.at[c, hh])
            for c, (hbm, buf) in enumerate(((kp_hbm, kp_ref), (vp_hbm, vp_ref)))
            for hh in range(N_HEADS)]
        for copy in past_copies:
            copy.start()
    else:
        (qb_ref, qib_ref, wit_ref, k_ref, vt_ref, kib_ref, o_ref,
         sc_ref, grp_ref, thr_ref, cnt_ref, acc_ref, m_ref, l_ref) = refs
        past_copies = []
    i = pl.program_id(1)
    Q = q_cols
    n_key_pad = sc_ref.shape[0]
    n_groups = grp_ref.shape[0]
    n_past_tiles = past_len // tk

    class NewKeys:
        def __init__(self, j):
            self.off = pl.multiple_of((j - n_past_tiles) * tk, tk)

        def idx(self):
            return kib_ref[0, pl.ds(self.off, tk), :IDX_DIM]

        def k(self, hh):
            return k_ref[0, pl.ds(self.off, tk), hh * HEAD_DIM:(hh + 1) * HEAD_DIM]

        def vt(self, hh):
            return vt_ref[0, hh * HEAD_DIM:(hh + 1) * HEAD_DIM, pl.ds(self.off, tk)]

    class PastKeys:
        def __init__(self, j):
            self.off = pl.multiple_of(j * tk, tk)

        def idx(self):
            return kip_ref[0, 0, pl.ds(self.off, tk), :].astype(BF16)

        def k(self, hh):
            return kp_ref[hh, pl.ds(self.off, tk), :].astype(BF16)

        def vt(self, hh):
            return vp_ref[hh, pl.ds(self.off, tk), :].T.astype(BF16)

    def for_key_tiles(n_tiles, body):
        if n_past_tiles:
            lax.fori_loop(0, jnp.minimum(n_tiles, n_past_tiles),
                          lambda j, c: (body(j, PastKeys(j)), c)[1], 0)
        lax.fori_loop(n_past_tiles, n_tiles, lambda j, c: (body(j, NewKeys(j)), c)[1], 0)

    lane = lax.broadcasted_iota(I32, (1, Q), 1)
    qpos = past_len + i * Q + lane
    lim = jnp.minimum((qpos // CHUNK + 1) * CHUNK, n_keys)
    last_lim = jnp.minimum(((past_len + i * Q + n_valid_q - 1) // CHUNK + 1) * CHUNK, n_keys)
    n_tiles = (last_lim + tk - 1) // tk
    row0 = lax.broadcasted_iota(I32, (tk, Q), 0)

    n_part = 4 * SUBLANES

    def key_of(x):
        bits = pltpu.bitcast(x, I32)
        return jnp.where(bits < 0, INT_MIN - bits, bits)

    def float_of(key):
        return pltpu.bitcast(jnp.where(key < 0, INT_MIN - key, key), F32)

    def floor_avg(a, b):
        return (a >> 1) + (b >> 1) + (a & b & 1)

    def count(pred):
        def body(j, c):
            off = pl.multiple_of(j * tk, tk)
            hit = jnp.where(pred(sc_ref[pl.ds(off, tk), :], row0 + off), 1, 0)
            return c + jnp.sum(hit.reshape(tk // n_part, n_part, Q), axis=0)
        c = lax.fori_loop(0, n_tiles, body, jnp.zeros((n_part, Q), I32))
        return jnp.sum(c, axis=0, keepdims=True)

    def scores_and_candidate(hi_ref, lo_ref):
        wsc = wit_ref[0] * (IDX_DIM ** -0.5 * N_IDX_HEADS ** -0.5)
        grp_ref[...] = jnp.full(grp_ref.shape, -jnp.inf, F32)

        def score_tile(j, src):
            off = pl.multiple_of(j * tk, tk)
            kt = src.idx()
            s = jnp.zeros((tk, Q), F32)
            for hh in range(N_IDX_HEADS):
                qh = qib_ref[0, :, hh * LANES:hh * LANES + IDX_DIM]
                d = lax.dot_general(kt, qh, NT_DIMS, preferred_element_type=F32)
                s = s + jnp.maximum(d, 0.0) * wsc[hh:hh + 1, :]
            row = row0 + off
            s = jnp.where(row < lim, s, NEG_INF)
            if n_key_pad != n_keys:
                s = jnp.where(row < n_keys, s, -jnp.inf)
            sc_ref[pl.ds(off, tk), :] = s
            key = key_of(s)
            hi_ref[pl.ds(off, tk), :] = (key >> HALF_BITS).astype(I16)
            lo_ref[pl.ds(off, tk), :] = ((key & (2 * HALF_BIAS - 1)) - HALF_BIAS).astype(I16)
            g = grp_ref[...]
            for c in range(tk // n_groups):
                g = jnp.maximum(g, s[c * n_groups:(c + 1) * n_groups])
            grp_ref[...] = g

        for_key_tiles(n_tiles, score_tile)

        def count16(plane_ref, cand):
            cand16 = cand.astype(I16)

            def body(j, c):
                off = pl.multiple_of(j * tk, tk)
                hit = jnp.where(plane_ref[pl.ds(off, tk), :] >= cand16, jnp.int16(1), jnp.int16(0))
                for g in range(tk // n_part):
                    c = c + hit[g * n_part:(g + 1) * n_part]
                return c

            c = lax.fori_loop(0, n_tiles, body, jnp.zeros((n_part, Q), I16))
            return jnp.sum(c.astype(I32), axis=0, keepdims=True)

        def bisect16(plane_ref, lo, hi, want, n_steps):
            def step(_, st):
                lo, hi, n_hi = st
                mid = (lo + hi) >> 1
                n = count16(plane_ref, mid)
                ge = n >= want
                return jnp.where(ge, mid, lo), jnp.where(ge, hi, mid), jnp.where(ge, n_hi, n)
            lo, _, n_hi = lax.fori_loop(0, n_steps, step, (lo, hi, jnp.zeros((1, Q), I32)))
            return lo, n_hi

        g = grp_ref[...]
        lo0 = key_of(jnp.min(g, axis=0, keepdims=True)) >> HALF_BITS
        hi0 = (key_of(jnp.max(g, axis=0, keepdims=True)) >> HALF_BITS) + 1
        n_steps = jnp.max(INT_BITS - lax.clz(hi0 - lo0 - 1))
        h_thr, n_above = bisect16(hi_ref, lo0, hi0, k_top, n_steps)
        h_thr16 = h_thr.astype(I16)

        def park(j, carry):
            off = pl.multiple_of(j * tk, tk)
            lo_ref[pl.ds(off, tk), :] = jnp.where(hi_ref[pl.ds(off, tk), :] == h_thr16,
                                                  lo_ref[pl.ds(off, tk), :], jnp.int16(-HALF_BIAS))
            return carry

        lax.fori_loop(0, n_tiles, park, 0)
        l_thr, _ = bisect16(lo_ref, jnp.full((1, Q), -HALF_BIAS, I32),
                            jnp.full((1, Q), HALF_BIAS, I32), k_top - n_above, HALF_BITS)
        cand = float_of(h_thr * (2 * HALF_BIAS) + (l_thr + HALF_BIAS))
        thr_ref[...] = jnp.broadcast_to(cand, thr_ref.shape)

    plane16 = pltpu.VMEM((n_key_pad, Q), I16)
    pl.run_scoped(scores_and_candidate, plane16, plane16)

    def bisect(lo, hi, n_steps):
        def step(_, st):
            lo, hi = st
            mid = floor_avg(lo, hi)
            t = float_of(mid)
            ge = count(lambda s, row: s >= t) >= k_top
            return jnp.where(ge, mid, lo), jnp.where(ge, hi, mid)
        return lax.fori_loop(0, n_steps, step, (lo, hi))[0]

    def count_ge_gt(t):
        def body(j, c):
            off = pl.multiple_of(j * tk, tk)
            s = sc_ref[pl.ds(off, tk), :]
            fold = lambda hit: jnp.sum(hit.reshape(tk // n_part, n_part, Q), axis=0)
            return c[0] + fold(jnp.where(s >= t, 1, 0)), c[1] + fold(jnp.where(s > t, 1, 0))
        z = jnp.zeros((n_part, Q), I32)
        ge, gt = lax.fori_loop(0, n_tiles, body, (z, z))
        return jnp.sum(ge, axis=0, keepdims=True), jnp.sum(gt, axis=0, keepdims=True)

    def publish(t, n_ge, n_gt):
        thr_ref[...] = jnp.broadcast_to(t, thr_ref.shape)
        cnt_ref[0] = jnp.broadcast_to(n_ge, cnt_ref.shape[1:])
        cnt_ref[1] = jnp.broadcast_to(n_gt, cnt_ref.shape[1:])

    valid_q = lane < n_valid_q
    cand = thr_ref[0:1, :]
    n_ge, n_gt = count_ge_gt(cand)
    publish(cand, n_ge, n_gt)
    refuted = jnp.max(jnp.where(((n_ge < k_top) | (n_gt >= k_top)) & valid_q, 1, 0)) > 0

    @pl.when(refuted)
    def _():
        def min_where(pred):
            def body(j, m):
                off = pl.multiple_of(j * tk, tk)
                s = sc_ref[pl.ds(off, tk), :]
                hit = jnp.where(pred(s), s, jnp.inf)
                return jnp.minimum(m, jnp.min(hit.reshape(tk // n_part, n_part, Q), axis=0))
            m = lax.fori_loop(0, n_tiles, body, jnp.full((n_part, Q), jnp.inf, F32))
            return jnp.min(m, axis=0, keepdims=True)

        full_lo = jnp.full((1, Q), -KEY_INF, I32)
        t_grid = float_of(bisect(full_lo, jnp.full((1, Q), KEY_INF + 1, I32), INT_BITS))
        t0 = min_where(lambda s: s >= t_grid)

        def crowded(st):
            return jnp.max(jnp.where(st[2] >= k_top, 1, 0)) > 0

        def raise_thr(st):
            t, _, n_gt = st
            t = jnp.where(n_gt >= k_top, min_where(lambda s: s > t), t)
            return (t,) + count_ge_gt(t)

        publish(*lax.while_loop(crowded, raise_thr, (t0,) + count_ge_gt(t0)))

    thr = thr_ref[0:1, :]
    n_ge, n_gt = cnt_ref[0, 0:1, :], cnt_ref[1, 0:1, :]

    untidy = jnp.max(jnp.where(((n_ge > k_top) | (thr <= NEG_INF)) & valid_q, 1, 0)) > 0

    @pl.when(untidy)
    def _():
        want = (k_top - n_gt).astype(F32)
        tri = jnp.where(lax.broadcasted_iota(I32, (LANES, LANES), 0)
                        >= lax.broadcasted_iota(I32, (LANES, LANES), 1), 1.0, 0.0).astype(BF16)

        def drop_tile(j, seen):
            off = pl.multiple_of(j * tk, tk)
            s = sc_ref[pl.ds(off, tk), :]
            tied = s == thr
            flag = jnp.where(tied, 1.0, 0.0).astype(BF16)
            local = [jnp.dot(tri, flag[c * LANES:(c + 1) * LANES], preferred_element_type=F32)
                     for c in range(tk // LANES)]
            ranks = []
            for r in local:
                ranks.append(r + seen)
                seen = ranks[-1][LANES - 1:LANES, :]
            rank = jnp.concatenate(ranks, axis=0)
            drop = (tied & (rank > want)) | (row0 + off >= lim)
            sc_ref[pl.ds(off, tk), :] = jnp.where(drop, -jnp.inf, s)
            return seen

        lax.fori_loop(0, n_tiles, drop_tile, jnp.zeros((1, Q), F32))

    for copy in past_copies:
        copy.wait()
    m_ref[...] = jnp.full(m_ref.shape, NEG_INF, F32)
    l_ref[...] = jnp.zeros(l_ref.shape, F32)
    acc_ref[...] = jnp.zeros(acc_ref.shape, F32)
    ones_rows = jnp.ones((2 * SUBLANES, tk), BF16)

    def attend(bias_ref, s_ref):
        def set_bias(j):
            off = pl.multiple_of(j * tk, tk)
            bias_ref[...] = jnp.where(sc_ref[pl.ds(off, tk), :] >= thr, 0.0, NEG_INF)

        def logits(src, hh):
            lo, hi = hh * HEAD_DIM, (hh + 1) * HEAD_DIM
            s = lax.dot_general(src.k(hh), qb_ref[0, :, lo:hi], NT_DIMS,
                                preferred_element_type=F32) + bias_ref[...]
            s_ref[hh] = s
            return jnp.maximum(m_ref[hh], jnp.max(s, axis=0, keepdims=True))

        def accumulate(src, hh, m_new):
            lo, hi = hh * HEAD_DIM, (hh + 1) * HEAD_DIM
            alpha = jnp.exp2(m_ref[hh] - m_new)
            p = jnp.exp2(s_ref[hh] - m_new[0:1])
            vt1 = jnp.concatenate([src.vt(hh), ones_rows], axis=0)
            pv = jnp.dot(vt1, p.astype(BF16), preferred_element_type=F32)
            l_ref[hh] = alpha * l_ref[hh] + pv[HEAD_DIM:HEAD_DIM + SUBLANES]
            acc_ref[lo:hi, :] = alpha[0:1] * acc_ref[lo:hi, :] + pv[:HEAD_DIM]
            m_ref[hh] = m_new

        def attn_tile(j, src):
            set_bias(j)
            m_news = [logits(src, hh) for hh in range(HEAD_LOOKAHEAD)]
            for hh in range(N_HEADS):
                if hh + HEAD_LOOKAHEAD < N_HEADS:
                    m_news.append(logits(src, hh + HEAD_LOOKAHEAD))
                accumulate(src, hh, m_news[hh])

        def attn_tile_pipelined(j, m_early):
            src, j_next = NewKeys(j), jnp.minimum(j + 1, n_tiles - 1)
            m_news, m_next = list(m_early), []
            for hh in range(N_HEADS):
                if hh + HEAD_LOOKAHEAD < N_HEADS:
                    m_news.append(logits(src, hh + HEAD_LOOKAHEAD))
                else:
                    if hh + HEAD_LOOKAHEAD == N_HEADS:
                        set_bias(j_next)
                    m_next.append(logits(NewKeys(j_next), hh + HEAD_LOOKAHEAD - N_HEADS))
                accumulate(src, hh, m_news[hh])
            return tuple(m_next)

        if n_past_tiles:
            for_key_tiles(n_tiles, attn_tile)
        else:
            set_bias(0)
            first = tuple(logits(NewKeys(0), hh) for hh in range(HEAD_LOOKAHEAD))
            lax.fori_loop(0, n_tiles, attn_tile_pipelined, first)

    pl.run_scoped(attend, pltpu.VMEM((tk, Q), F32), pltpu.VMEM((N_HEADS, tk, Q), F32))

    for hh in range(N_HEADS):
        lo, hi = hh * HEAD_DIM, (hh + 1) * HEAD_DIM
        acc_ref[lo:hi, :] = acc_ref[lo:hi, :] / l_ref[hh, 0:1, :]
    o_ref[0] = acc_ref[...].T


def _attn_call(qb, qib, wit, k_new, vt_new, kib_new, past, layer, *, q_cols, n_valid_q, tk,
               n_keys, k_top):
    nb, t_all, width = qb.shape
    past_len = past[0].shape[2] if past else 0
    n_new_pad = k_new.shape[1]
    n_key_pad = past_len + n_new_pad
    n_groups = 2 * LANES
    assert t_all % q_cols == 0 and n_new_pad % tk == 0 and past_len % tk == 0
    assert tk % n_groups == 0 and n_groups >= k_top and n_key_pad >= n_keys
    nq = t_all // q_cols
    q_blk = lambda w_: pl.BlockSpec((1, q_cols, w_), lambda b, i: (b, i, 0))
    keys_blk = _resident if nq > 1 else pl.BlockSpec
    stream_blk = lambda a: keys_blk((1,) + a.shape[1:], lambda b, i: (b, 0, 0))
    return pl.pallas_call(
        functools.partial(_attn_kernel, q_cols=q_cols, n_valid_q=n_valid_q, tk=tk,
                          past_len=past_len, n_keys=n_keys, k_top=k_top, layer=layer),
        grid=(nb, nq),
        in_specs=[q_blk(width), q_blk(N_IDX_HEADS * LANES),
                  pl.BlockSpec((1, N_IDX_HEADS, q_cols), lambda b, i: (b, 0, i)),
                  stream_blk(k_new), stream_blk(vt_new), stream_blk(kib_new)]
                 + [pl.BlockSpec(memory_space=pl.ANY) for a in past[:2]]
                 + [pl.BlockSpec((1, 1) + a.shape[2:],
                                 lambda b, i, nd=a.ndim: (layer, b) + (0,) * (nd - 2)) for a in past[2:]],
        out_specs=q_blk(width),
        out_shape=jax.ShapeDtypeStruct((nb, t_all, width), F32),
        scratch_shapes=[pltpu.VMEM((n_key_pad, q_cols), F32),
                        pltpu.VMEM((n_groups, q_cols), F32),
                        pltpu.VMEM((SUBLANES, q_cols), F32),
                        pltpu.VMEM((2, SUBLANES, q_cols), I32),
                        pltpu.VMEM((width, q_cols), F32),
                        pltpu.VMEM((N_HEADS, SUBLANES, q_cols), F32),
                        pltpu.VMEM((N_HEADS, SUBLANES, q_cols), F32)]
                       + ([pltpu.VMEM((N_HEADS, past_len, HEAD_DIM), F32),
                           pltpu.VMEM((N_HEADS, past_len, HEAD_DIM), F32),
                           pltpu.SemaphoreType.DMA((2, N_HEADS))] if past else []),
        compiler_params=pltpu.CompilerParams(
            dimension_semantics=("arbitrary", "arbitrary"), vmem_limit_bytes=VMEM_LIMIT_BYTES),
        name="attn",
    )(qb, qib, wit, k_new, vt_new, kib_new, *past)


def _post_kernel(attn_ref, sza_ref, ga_ref, gyb_ref, x_ref, waout_ref, wout_ref, npost_ref, y_ref):
    ya_in = (sza_ref[...] * attn_ref[...]).astype(BF16)
    y_a = jnp.dot(ya_in, waout_ref[...], preferred_element_type=F32)
    merged = ga_ref[...] * y_a + gyb_ref[...]
    out = jnp.dot(merged.astype(BF16), wout_ref[...], preferred_element_type=F32)
    ms = jnp.mean(out * out, axis=-1, keepdims=True)
    y_ref[...] = x_ref[...] + (out * lax.rsqrt(ms + EPS)) * npost_ref[...]


def _post_call(attn, sza, ga, gyb, x, w, *, rows):
    n, d = x.shape
    assert n % rows == 0
    row_blk = pl.BlockSpec((rows, d), lambda t: (t, 0))
    full = lambda a: _resident(a.shape, lambda t: (0,) * a.ndim)
    weights = (w["w_a_out"], w["w_out"], w["norm_post"])
    return pl.pallas_call(
        _post_kernel,
        grid=(n // rows,),
        in_specs=[row_blk] * 5 + [full(a) for a in weights],
        out_specs=row_blk,
        out_shape=jax.ShapeDtypeStruct((n, d), F32),
        compiler_params=pltpu.CompilerParams(
            dimension_semantics=("arbitrary",), vmem_limit_bytes=VMEM_LIMIT_BYTES),
        name="post",
    )(attn, sza, ga, gyb, x, *weights)


def _prep_weights(norm_pre, w_in, conv_w, w_a_out, w_b_out, b_merge, w_out, norm_post):
    W, C, D = ATTN_WIDTH, CONV_WIDTH, D_MODEL
    o_qi = 4 * W
    o_ki = o_qi + N_IDX_HEADS * IDX_DIM
    o_wi = o_ki + IDX_DIM
    o_conv = o_wi + N_IDX_HEADS
    o_g = o_conv + 4 * C
    bf = lambda a: a.astype(BF16)
    w_qi = w_in[:, o_qi:o_ki].reshape(D, N_IDX_HEADS, IDX_DIM)
    w_qi = jnp.pad(w_qi, ((0, 0), (0, 0), (0, LANES - IDX_DIM))).reshape(D, N_IDX_HEADS * LANES)
    return {
        "norm_pre": norm_pre.reshape(1, D),
        "w_qkvz": bf(w_in[:, :o_qi]),
        "w_qi": bf(w_qi),
        "w_ki": bf(jnp.pad(w_in[:, o_ki:o_wi], ((0, 0), (0, LANES - IDX_DIM)))),
        "w_wit": bf(jnp.pad(w_in[:, o_wi:o_conv].T, ((0, LANES - N_IDX_HEADS), (0, 0)))),
        "w_conv": bf(w_in[:, o_conv:o_g]),
        "w_g": bf(w_in[:, o_g:]),
        "conv_w": conv_w,
        "w_b_out": bf(w_b_out),
        "b_merge": b_merge,
        "w_a_out": bf(w_a_out),
        "w_out": bf(w_out),
        "norm_post": norm_post.reshape(1, D),
    }


def _round_up(a, b):
    return (a + b - 1) // b * b


def _hybrid_layer(x, past, layer, conv_past, w, *, flatten_streams):
    B, T, D = x.shape
    P = past[0].shape[2] if past else 0
    n_keys = P + T
    k_top = min(TOPK_MAX, n_keys // 4)

    if flatten_streams:
        outs = _proj_call(x.reshape(1, B * T, D), conv_past, w, groups=B, rows=T, n_row_tiles=1)
    else:
        rows = min(T, 256)
        outs = _proj_call(x, conv_past, w, groups=1, rows=rows, n_row_tiles=T // rows)
    k, v, ki, cstate, qb, kb, vtb, qib, kib, wit, sza, ga, gyb = outs
    per_stream = lambda a: a.reshape(B, T, a.shape[-1])
    k, v, ki, qb, kb, qib, kib = map(per_stream, (k, v, ki, qb, kb, qib, kib))
    if flatten_streams:
        per_stream_t = lambda a: a.reshape(a.shape[1], B, T).transpose(1, 0, 2)
        vtb, wit = per_stream_t(vtb), per_stream_t(wit)

    q_cols = min(_round_up(T, LANES), 2 * LANES)
    t_pad = _round_up(T, q_cols)
    tk = 512 if T % 512 == 0 and P % 512 == 0 else 2 * LANES
    n_new_pad = _round_up(T, tk)
    if n_new_pad != T:
        pad_rows = ((0, 0), (0, n_new_pad - T), (0, 0))
        kb, kib = jnp.pad(kb, pad_rows), jnp.pad(kib, pad_rows)
        vtb = jnp.pad(vtb, ((0, 0), (0, 0), (0, n_new_pad - T)))
    if t_pad != T:
        pad_q = ((0, 0), (0, t_pad - T), (0, 0))
        qb, qib = jnp.pad(qb, pad_q), jnp.pad(qib, pad_q)
        wit = jnp.pad(wit, ((0, 0), (0, 0), (0, t_pad - T)))
    attn = _attn_call(qb, qib, wit, kb, vtb, kib, past, layer, q_cols=q_cols,
                      n_valid_q=min(T, q_cols), tk=tk, n_keys=n_keys, k_top=k_top)
    attn = attn[:, :T]

    flat = lambda a: a.reshape(B * T, a.shape[-1])
    y = _post_call(flat(attn), flat(sza), flat(ga), flat(gyb), flat(x), w, rows=min(B * T, 512))
    return (y.reshape(B, T, D), k.reshape(B, T, N_HEADS, HEAD_DIM),
            v.reshape(B, T, N_HEADS, HEAD_DIM), ki, cstate)


def kernel(x_prompt, x_sample, cache_k, cache_v, cache_kidx, state_conv, norm_pre, w_in, conv_w,
           w_a_out, w_b_out, b_merge, w_out, norm_post):
    B = x_prompt.shape[0]
    zero_conv = jnp.zeros((B, CONV_K - 1, CONV_WIDTH), x_prompt.dtype)
    hp, hs = x_prompt, x_sample
    caches = (cache_k, cache_v, cache_kidx)
    per_layer = []
    for layer in range(w_in.shape[0]):
        w = _prep_weights(norm_pre[layer], w_in[layer], conv_w[layer], w_a_out[layer],
                          w_b_out[layer], b_merge[layer], w_out[layer], norm_post[layer])
        hp, k1, v1, ki1, c1 = _hybrid_layer(hp, (), 0, zero_conv, w, flatten_streams=False)
        hs, k2, v2, ki2, c2 = _hybrid_layer(hs, caches, layer, state_conv[layer], w,
                                            flatten_streams=True)
        per_layer.append((k1, v1, ki1, c1, k2, v2, ki2, c2))
    return (hp, hs) + tuple(jnp.stack(leaf) for leaf in zip(*per_layer))
```
